```python
import math
import jax, jax.numpy as jnp
from jax import lax
import numpy as np

D_MODEL = 2048
BATCH = 8
SEQ = 2048
DEPTH = 1

GLA_HEADS = 4
GLA_DK = D_MODEL // 2
GLA_DV = D_MODEL
GLA_HK = GLA_DK // GLA_HEADS
GLA_HV = GLA_DV // GLA_HEADS
GLA_RANK = 16
GLA_TAU = 16.0
GLA_CHUNK = 64
FOX_HEADS = 16
FOX_HD = D_MODEL // FOX_HEADS
FOX_BLOCK = 128
FOX_W = FOX_HEADS * FOX_HD
D_FF = 4 * D_MODEL
N_MOD = 6
EPS = 1e-6

IN_SIZES = (GLA_DK, GLA_DK, GLA_DV, GLA_DV, GLA_RANK,
            FOX_W, FOX_W, FOX_W, FOX_HEADS,
            D_MODEL, D_MODEL)
D_IN = 2 * GLA_DK + 2 * GLA_DV + GLA_RANK + 3 * FOX_W + FOX_HEADS + 2 * D_MODEL

kernel_name = "hybrid_gla_fox_adaln_block"


def _split_points():
    return [int(v) for v in np.cumsum(np.array(IN_SIZES))[:-1]]


def rms_norm(x, g):
    xf = x.astype(jnp.float32)
    y = xf * lax.rsqrt(jnp.mean(xf * xf, axis=-1, keepdims=True) + EPS)
    return (y * g.astype(jnp.float32)).astype(x.dtype)


def to_heads(t, n):
    b, s, _ = t.shape
    return t.reshape(b, s, n, -1).transpose(0, 2, 1, 3)


def merge_heads(t):
    b, h, s, d = t.shape
    return t.transpose(0, 2, 1, 3).reshape(b, s, h * d)


def gla_chunked(q, k, v, log_a):
    B, H, S, dk = q.shape
    dv = v.shape[-1]
    n_chunks = S // GLA_CHUNK

    def chunks(t):
        return t.astype(jnp.float32).reshape(B, H, n_chunks, GLA_CHUNK, t.shape[-1]).transpose(2, 0, 1, 3, 4)

    causal = jnp.tril(jnp.ones((GLA_CHUNK, GLA_CHUNK), dtype=bool))

    def step(state, inp):
        qc, kc, vc, lac = inp
        b = jnp.cumsum(lac, axis=-2)
        b_last = b[..., -1:, :]
        q_dec = qc * jnp.exp(b)
        attn = jnp.einsum('bhtd,bhsd->bhts', q_dec, kc * jnp.exp(-b))
        attn = jnp.where(causal, attn, 0.0)
        o = jnp.einsum('bhts,bhsv->bhtv', attn, vc) + jnp.einsum('bhtd,bhdv->bhtv', q_dec, state)
        k_dec = kc * jnp.exp(b_last - b)
        state = state * jnp.swapaxes(jnp.exp(b_last), -1, -2) + jnp.einsum('bhsd,bhsv->bhdv', k_dec, vc)
        return state, o

    state0 = jnp.zeros((B, H, dk, dv), jnp.float32)
    _, o = lax.scan(step, state0, (chunks(q), chunks(k), chunks(v), chunks(log_a)))
    return o.transpose(1, 2, 0, 3, 4).reshape(B, H, S, dv).astype(v.dtype)


def forgetting_attention(q, k, v, log_f):
    S = q.shape[2]
    scale = 1.0 / math.sqrt(q.shape[-1])
    cum = jnp.cumsum(log_f, axis=-1)
    outs = []
    for i in range(S // FOX_BLOCK):
        q0, end = i * FOX_BLOCK, (i + 1) * FOX_BLOCK
        qb = q[:, :, q0:end]
        kb, vb = k[:, :, :end], v[:, :, :end]
        s = jnp.einsum('bhtd,bhsd->bhts', qb, kb).astype(jnp.float32) * scale
        s = s + cum[:, :, q0:end, None] - cum[:, :, None, :end]
        qpos = jnp.arange(q0, end)[:, None]
        kpos = jnp.arange(end)[None, :]
        s = jnp.where(kpos <= qpos, s, -jnp.inf)
        p = jax.nn.softmax(s, axis=-1).astype(v.dtype)
        outs.append(jnp.einsum('bhts,bhsd->bhtd', p, vb))
    return jnp.concatenate(outs, axis=2)


def setup_inputs(seed: int = 0) -> dict:
    key = jax.random.key(seed)
    ks = jax.random.split(key, 20)
    f32 = jnp.float32
    nrm = lambda k, shape, s: jax.random.normal(k, shape, f32) * s
    D = D_MODEL
    return {
        "x": nrm(ks[0], (BATCH, SEQ, D), 1.0),
        "c": nrm(ks[1], (BATCH, D), 1.0),
        "w_ada": nrm(ks[2], (DEPTH, D, N_MOD * D), 0.5 * D ** -0.5),
        "b_ada": nrm(ks[3], (DEPTH, N_MOD * D), 0.01),
        "g_norm1": 1.0 + nrm(ks[4], (DEPTH, D), 0.05),
        "w_in": nrm(ks[5], (DEPTH, D, D_IN), D ** -0.5),
        "w_gla_a_up": nrm(ks[6], (DEPTH, GLA_RANK, GLA_DK), GLA_RANK ** -0.5),
        "b_gla_a": 1.0 + nrm(ks[7], (DEPTH, GLA_DK), 0.5),
        "g_gla_out": 1.0 + nrm(ks[8], (DEPTH, GLA_HV), 0.05),
        "b_fox_f": 2.0 + nrm(ks[9], (DEPTH, FOX_HEADS), 1.0),
        "w_proj_gla": nrm(ks[10], (DEPTH, GLA_DV, D), GLA_DV ** -0.5),
        "w_proj_fox": nrm(ks[11], (DEPTH, FOX_W, D), FOX_W ** -0.5),
        "w_out": nrm(ks[12], (DEPTH, D, D), D ** -0.5),
        "g_norm2": 1.0 + nrm(ks[13], (DEPTH, D), 0.05),
        "w_ff1": nrm(ks[14], (DEPTH, D, D_FF), D ** -0.5),
        "w_ff2": nrm(ks[15], (DEPTH, D_FF, D), D_FF ** -0.5),
        "g_final": 1.0 + nrm(ks[16], (D,), 0.05),
    }


def reference(x, c, w_ada, b_ada, g_norm1, w_in, w_gla_a_up, b_gla_a, g_gla_out, b_fox_f,
              w_proj_gla, w_proj_fox, w_out, g_norm2, w_ff1, w_ff2, g_final):
    cond = jax.nn.silu(c)
    split_pts = _split_points()
    for l in range(DEPTH):
        mod = (cond @ w_ada[l] + b_ada[l])[:, None, :]
        shift1, scale1, gate1, shift2, scale2, gate2 = jnp.split(mod, N_MOD, axis=-1)

        h = rms_norm(x, g_norm1[l]) * (1.0 + scale1) + shift1
        proj = h @ w_in[l]
        (gq, gk, gv, gg, ga_low, fq, fk, fv, f_logit, gate_a, gate_b) = jnp.split(proj, split_pts, axis=-1)

        log_a = jax.nn.log_sigmoid((ga_low @ w_gla_a_up[l] + b_gla_a[l]).astype(jnp.float32)) / GLA_TAU
        o_a = gla_chunked(to_heads(gq * (GLA_HK ** -0.5), GLA_HEADS), to_heads(gk, GLA_HEADS),
                          to_heads(gv, GLA_HEADS), to_heads(log_a, GLA_HEADS))
        o_a = merge_heads(rms_norm(o_a, g_gla_out[l])) * jax.nn.silu(gg)

        log_f = jax.nn.log_sigmoid((f_logit + b_fox_f[l]).astype(jnp.float32)).transpose(0, 2, 1)
        o_b = merge_heads(forgetting_attention(to_heads(fq, FOX_HEADS), to_heads(fk, FOX_HEADS),
                                               to_heads(fv, FOX_HEADS), log_f))

        merged = jax.nn.sigmoid(gate_a) * (o_a @ w_proj_gla[l]) + jax.nn.sigmoid(gate_b) * (o_b @ w_proj_fox[l])
        x = x + gate1 * (merged @ w_out[l])

        h2 = rms_norm(x, g_norm2[l]) * (1.0 + scale2) + shift2
        x = x + gate2 * (jnp.square(jax.nn.relu(h2 @ w_ff1[l])) @ w_ff2[l])
    return rms_norm(x, g_final)
```

```python
import functools
import math

import jax
import jax.numpy as jnp
from jax import lax
from jax.experimental import pallas as pl
from jax.experimental.pallas import tpu as pltpu

F32 = jnp.float32
BF16 = jnp.bfloat16

EPS = 1e-6
LANES = 128
GLA_HEADS = 4
GLA_RANK = 16
GLA_TAU = 16.0
GLA_CHUNK = 64
FOX_HEADS = 16
N_MOD = 6
NEG_BIG = -1e30
VMEM_LIMIT = 56 * 1024 * 1024

_NT = (((1,), (1,)), ((), ()))
_TN = (((0,), (0,)), ((), ()))


def _dot(a, b):
    return jnp.dot(a, b, preferred_element_type=F32)


def _log_sigmoid(z):
    return jnp.minimum(z, 0.0) - jnp.log(1.0 + jnp.exp(-jnp.abs(z)))


def _silu(z):
    return z * jax.nn.sigmoid(z)


def _split_bf16(a):
    hi = a.astype(BF16)
    lo = (a - hi.astype(F32)).astype(BF16)
    return hi, lo


def _params(sem):
    return pltpu.CompilerParams(dimension_semantics=sem, vmem_limit_bytes=VMEM_LIMIT)


def _ada_kernel(c_ref, w_ref, b_ref, o_ref):
    cond = _silu(c_ref[...]).astype(BF16)
    o_ref[...] = _dot(cond, w_ref[...].astype(BF16)) + b_ref[...]


def _ada(c, w_ada, b_ada, tn=1024):
    B, D = c.shape
    N = w_ada.shape[1]
    return pl.pallas_call(
        _ada_kernel,
        out_shape=jax.ShapeDtypeStruct((B, N), F32),
        grid=(N // tn,),
        in_specs=[pl.BlockSpec((B, D), lambda j: (0, 0)),
                  pl.BlockSpec((D, tn), lambda j: (0, j)),
                  pl.BlockSpec((1, tn), lambda j: (0, j))],
        out_specs=pl.BlockSpec((B, tn), lambda j: (0, j)),
        compiler_params=_params(("arbitrary",)),
        name="ada",
    )(c, w_ada, b_ada)


def _modulated_norm_rows(x, g, scale, shift):
    y = x * lax.rsqrt(jnp.mean(x * x, axis=-1, keepdims=True) + EPS) * g
    return y * (1.0 + scale) + shift


def _in_kernel(x_ref, g_ref, scale_ref, shift_ref, w_ref, wlow_ref, proj_ref, low_ref, h_ref, *, slab):
    @pl.when(pl.program_id(1) == 0)
    def _():
        g = g_ref[...]
        scale = scale_ref[...]
        shift = shift_ref[...]

        def body(s, carry):
            r = pl.ds(pl.multiple_of(s * slab, slab), slab)
            h_ref[r, :] = _modulated_norm_rows(x_ref[r, :], g, scale, shift).astype(BF16)
            return carry

        lax.fori_loop(0, x_ref.shape[0] // slab, body, 0)
        low_ref[...] = _dot(h_ref[...], wlow_ref[...])

    proj_ref[...] = _dot(h_ref[...], w_ref[...]).astype(proj_ref.dtype)


def _in_proj(x2d, g1, mod4, w_main, w_low, S, tm=1024, tn=1024, slab=16):
    T, D = x2d.shape
    N = w_main.shape[1]
    NL = w_low.shape[1]
    per_b = S // tm
    return pl.pallas_call(
        functools.partial(_in_kernel, slab=slab),
        out_shape=(jax.ShapeDtypeStruct((T, N), BF16), jax.ShapeDtypeStruct((T, NL), F32)),
        grid=(T // tm, N // tn),
        in_specs=[pl.BlockSpec((tm, D), lambda i, j: (i, 0)),
                  pl.BlockSpec((1, D), lambda i, j: (0, 0)),
                  pl.BlockSpec((None, None, 1, D), lambda i, j: (i // per_b, 1, 0, 0)),
                  pl.BlockSpec((None, None, 1, D), lambda i, j: (i // per_b, 0, 0, 0)),
                  pl.BlockSpec((D, tn), lambda i, j: (0, j)),
                  pl.BlockSpec((D, NL), lambda i, j: (0, 0))],
        out_specs=(pl.BlockSpec((tm, tn), lambda i, j: (i, j)),
                   pl.BlockSpec((tm, NL), lambda i, j: (i, 0))),
        scratch_shapes=[pltpu.VMEM((tm, D), BF16)],
        compiler_params=_params(("parallel", "arbitrary")),
        name="in_proj",
    )(x2d, g1, mod4, mod4, w_main, w_low)


def _gla_kernel(q_ref, k_ref, v_ref, gg_ref, low_ref, wup_ref, ba_ref, g_ref, o_ref, st_ref, *, q_scale):
    C = GLA_CHUNK
    S = q_ref.shape[0]
    st_ref[...] = jnp.zeros_like(st_ref)
    row = lax.broadcasted_iota(jnp.int32, (C, C), 0)
    col = lax.broadcasted_iota(jnp.int32, (C, C), 1)
    causal = col <= row
    tril = causal.astype(BF16)
    wup = wup_ref[...]
    ba = ba_ref[...]
    g = g_ref[...]

    def body(c, carry):
        r = pl.ds(pl.multiple_of(c * C, C), C)
        q = q_ref[r, :].astype(F32) * q_scale
        k = k_ref[r, :].astype(F32)
        v = v_ref[r, :]
        z = _dot(low_ref[r, :].astype(BF16), wup) + ba
        la = _log_sigmoid(z) * (1.0 / GLA_TAU)
        la_hi, la_lo = _split_bf16(la)
        b = _dot(tril, la_hi) + _dot(tril, la_lo)
        b_last = b[C - 1:C, :]
        q_dec = (q * jnp.exp(b)).astype(BF16)
        k_inv = (k * jnp.exp(-b)).astype(BF16)
        k_dec = (k * jnp.exp(b_last - b)).astype(BF16)
        attn = lax.dot_general(q_dec, k_inv, _NT, preferred_element_type=F32)
        attn = jnp.where(causal, attn, 0.0).astype(BF16)
        st = st_ref[...]
        o = _dot(attn, v) + lax.dot_general(q_dec, st.astype(BF16), _NT, preferred_element_type=F32)
        st_ref[...] = st * jnp.exp(b_last) + lax.dot_general(v, k_dec, _TN, preferred_element_type=F32)
        y = o * lax.rsqrt(jnp.mean(o * o, axis=-1, keepdims=True) + EPS) * g
        o_ref[r, :] = (y * _silu(gg_ref[r, :].astype(F32))).astype(o_ref.dtype)
        return carry

    lax.fori_loop(0, S // C, body, 0)


def _gla(proj, low, wup_pad, b_a, g_out, B, S, D):
    T = B * S
    H = GLA_HEADS
    dk = (D // 2) // H
    dv = D // H
    k_blk0 = (D // 2) // dk
    v_blk0 = D // dv
    gg_blk0 = 2 * D // dv
    return pl.pallas_call(
        functools.partial(_gla_kernel, q_scale=dk ** -0.5),
        out_shape=jax.ShapeDtypeStruct((T, D), BF16),
        grid=(B, H),
        in_specs=[pl.BlockSpec((S, dk), lambda b, h: (b, h)),
                  pl.BlockSpec((S, dk), lambda b, h: (b, k_blk0 + h)),
                  pl.BlockSpec((S, dv), lambda b, h: (b, v_blk0 + h)),
                  pl.BlockSpec((S, dv), lambda b, h: (b, gg_blk0 + h)),
                  pl.BlockSpec((S, LANES), lambda b, h: (b, 0)),
                  pl.BlockSpec((LANES, dk), lambda b, h: (0, h)),
                  pl.BlockSpec((1, dk), lambda b, h: (0, h)),
                  pl.BlockSpec((1, dv), lambda b, h: (0, 0))],
        out_specs=pl.BlockSpec((S, dv), lambda b, h: (b, h)),
        scratch_shapes=[pltpu.VMEM((dv, dk), F32)],
        compiler_params=_params(("parallel", "arbitrary")),
        name="gla",
    )(proj, proj, proj, proj, low, wup_pad, b_a, g_out)


def _cum_kernel(low_ref, bf_ref, col_ref, row_ref, *, blk):
    S = low_ref.shape[0]
    n_heads = row_ref.shape[0]
    r_i = lax.broadcasted_iota(jnp.int32, (blk, blk), 0)
    c_i = lax.broadcasted_iota(jnp.int32, (blk, blk), 1)
    tril = (c_i <= r_i).astype(BF16)
    triu = (r_i <= c_i).astype(BF16)
    bf = bf_ref[...]
    carry_row = jnp.zeros((1, LANES), F32)
    carry_col = jnp.zeros((LANES, 1), F32)
    for i in range(S // blk):
        r = slice(i * blk, (i + 1) * blk)
        lf = _log_sigmoid(low_ref[r, :] + bf)
        hi, lo = _split_bf16(lf)
        loc = _dot(tril, hi) + _dot(tril, lo) + carry_row
        col_ref[r, :] = loc
        carry_row = loc[blk - 1:blk, :]
        loc_t = (lax.dot_general(hi, triu, _TN, preferred_element_type=F32)
                 + lax.dot_general(lo, triu, _TN, preferred_element_type=F32) + carry_col)
        row_ref[:, r] = loc_t[:n_heads, :]
        carry_col = loc_t[:, blk - 1:blk]


def _fox_cum(low, bf_pad, B, S, blk=128):
    return pl.pallas_call(
        functools.partial(_cum_kernel, blk=blk),
        out_shape=(jax.ShapeDtypeStruct((B, S, LANES), F32),
                   jax.ShapeDtypeStruct((B, FOX_HEADS, S), F32)),
        grid=(B,),
        in_specs=[pl.BlockSpec((S, LANES), lambda b: (b, 1)),
                  pl.BlockSpec((1, LANES), lambda b: (0, 0))],
        out_specs=(pl.BlockSpec((None, S, LANES), lambda b: (b, 0, 0)),
                   pl.BlockSpec((None, FOX_HEADS, S), lambda b: (b, 0, 0))),
        compiler_params=_params(("arbitrary",)),
        name="fox_cum",
    )(low, bf_pad)


def _fox_kernel(q_ref, k_ref, v_ref, cc_ref, cr_ref, o_ref, *, blk, scale):
    S, hd = q_ref.shape
    h = pl.program_id(1)
    lane = lax.broadcasted_iota(jnp.int32, (blk, LANES), 1)
    row = lax.broadcasted_iota(jnp.int32, (blk, blk), 0)
    col = lax.broadcasted_iota(jnp.int32, (blk, blk), 1)
    causal = col <= row

    def q_block(qi, carry):
        qs = pl.ds(pl.multiple_of(qi * blk, blk), blk)
        q = q_ref[qs, :]
        cq = jnp.sum(jnp.where(lane == h, cc_ref[qs, :], 0.0), axis=-1, keepdims=True)

        def kv_step(j, state, diagonal):
            m, l, acc = state
            ks = pl.ds(pl.multiple_of(j * blk, blk), blk)
            s = lax.dot_general(q, k_ref[ks, :], _NT, preferred_element_type=F32) * scale
            s = s + cq - cr_ref[pl.ds(h, 1), ks]
            if diagonal:
                s = jnp.where(causal, s, NEG_BIG)
            m_new = jnp.maximum(m, jnp.max(s, axis=-1, keepdims=True))
            alpha = jnp.exp(m - m_new)
            p = jnp.exp(s - m_new)
            l = alpha * l + jnp.sum(p, axis=-1, keepdims=True)
            acc = alpha * acc + _dot(p.astype(BF16), v_ref[ks, :])
            return m_new, l, acc

        init = (jnp.full((blk, 1), NEG_BIG, F32), jnp.zeros((blk, 1), F32), jnp.zeros((blk, hd), F32))
        state = lax.fori_loop(0, qi, lambda j, st: kv_step(j, st, False), init)
        _, l, acc = kv_step(qi, state, True)
        o_ref[qs, :] = (acc / l).astype(o_ref.dtype)
        return carry

    lax.fori_loop(0, S // blk, q_block, 0)


def _fox(proj, cum_col, cum_row, B, S, D, blk=256):
    T = B * S
    H = FOX_HEADS
    hd = D // H
    q_blk0, k_blk0, v_blk0 = 3 * D // hd, 4 * D // hd, 5 * D // hd
    return pl.pallas_call(
        functools.partial(_fox_kernel, blk=blk, scale=1.0 / math.sqrt(hd)),
        out_shape=jax.ShapeDtypeStruct((T, D), BF16),
        grid=(B, H),
        in_specs=[pl.BlockSpec((S, hd), lambda b, h: (b, q_blk0 + h)),
                  pl.BlockSpec((S, hd), lambda b, h: (b, k_blk0 + h)),
                  pl.BlockSpec((S, hd), lambda b, h: (b, v_blk0 + h)),
                  pl.BlockSpec((None, S, LANES), lambda b, h: (b, 0, 0)),
                  pl.BlockSpec((None, H, S), lambda b, h: (b, 0, 0))],
        out_specs=pl.BlockSpec((S, hd), lambda b, h: (b, h)),
        compiler_params=_params(("parallel", "arbitrary")),
        name="fox",
    )(proj, proj, proj, cum_col, cum_row)


def _merge_kernel(oa_ref, ob_ref, wa_ref, wb_ref, ga_ref, gb_ref, o_ref):
    a = _dot(oa_ref[...], wa_ref[...])
    b = _dot(ob_ref[...], wb_ref[...])
    merged = jax.nn.sigmoid(ga_ref[...].astype(F32)) * a + jax.nn.sigmoid(gb_ref[...].astype(F32)) * b
    o_ref[...] = merged.astype(o_ref.dtype)


def _merge(o_a, o_b, w_a, w_b, proj, D, tm=512, tn=1024):
    T = o_a.shape[0]
    ga_blk0, gb_blk0 = 6 * D // tn, 7 * D // tn
    return pl.pallas_call(
        _merge_kernel,
        out_shape=jax.ShapeDtypeStruct((T, D), BF16),
        grid=(D // tn, T // tm),
        in_specs=[pl.BlockSpec((tm, D), lambda j, i: (i, 0)),
                  pl.BlockSpec((tm, D), lambda j, i: (i, 0)),
                  pl.BlockSpec((D, tn), lambda j, i: (0, j)),
                  pl.BlockSpec((D, tn), lambda j, i: (0, j)),
                  pl.BlockSpec((tm, tn), lambda j, i: (i, ga_blk0 + j)),
                  pl.BlockSpec((tm, tn), lambda j, i: (i, gb_blk0 + j))],
        out_specs=pl.BlockSpec((tm, tn), lambda j, i: (i, j)),
        compiler_params=_params(("arbitrary", "parallel")),
        name="merge",
    )(o_a, o_b, w_a, w_b, proj, proj)


def _out_kernel(m_ref, w_ref, x_ref, gate_ref, g_ref, scale_ref, shift_ref, x1_ref, h2_ref, *, slab):
    x1_ref[...] = _dot(m_ref[...], w_ref[...])
    gate = gate_ref[...]
    g = g_ref[...]
    scale = scale_ref[...]
    shift = shift_ref[...]

    def body(s, carry):
        r = pl.ds(pl.multiple_of(s * slab, slab), slab)
        x1 = x_ref[r, :] + gate * x1_ref[r, :]
        x1_ref[r, :] = x1
        h2_ref[r, :] = _modulated_norm_rows(x1, g, scale, shift).astype(BF16)
        return carry

    lax.fori_loop(0, x_ref.shape[0] // slab, body, 0)


def _out_proj(merged, w_out, x2d, mod4, g2, S, tm=512, slab=16):
    T, D = x2d.shape
    per_b = S // tm
    mod_spec = lambda k: pl.BlockSpec((None, None, 1, D), lambda i: (i // per_b, k, 0, 0))
    return pl.pallas_call(
        functools.partial(_out_kernel, slab=slab),
        out_shape=(jax.ShapeDtypeStruct((T, D), F32), jax.ShapeDtypeStruct((T, D), BF16)),
        grid=(T // tm,),
        in_specs=[pl.BlockSpec((tm, D), lambda i: (i, 0)),
                  pl.BlockSpec((D, D), lambda i: (0, 0)),
                  pl.BlockSpec((tm, D), lambda i: (i, 0)),
                  mod_spec(2),
                  pl.BlockSpec((1, D), lambda i: (0, 0)),
                  mod_spec(4),
                  mod_spec(3)],
        out_specs=(pl.BlockSpec((tm, D), lambda i: (i, 0)),
                   pl.BlockSpec((tm, D), lambda i: (i, 0))),
        compiler_params=_params(("parallel",)),
        name="out_proj",
    )(merged, w_out, x2d, mod4, g2, mod4, mod4)


def _ffn_kernel(h_ref, w1_ref, w2_ref, x1_ref, gate_ref, g_ref, o_ref, acc_ref, *, slab):
    f = pl.program_id(1)

    @pl.when(f == 0)
    def _():
        acc_ref[...] = jnp.zeros_like(acc_ref)

    a = jnp.maximum(_dot(h_ref[...], w1_ref[...]), 0.0)
    acc_ref[...] += _dot((a * a).astype(BF16), w2_ref[...])

    @pl.when(f == pl.num_programs(1) - 1)
    def _():
        gate = gate_ref[...]
        g = g_ref[...]

        def body(s, carry):
            r = pl.ds(pl.multiple_of(s * slab, slab), slab)
            x2 = x1_ref[r, :] + gate * acc_ref[r, :]
            o_ref[r, :] = x2 * lax.rsqrt(jnp.mean(x2 * x2, axis=-1, keepdims=True) + EPS) * g
            return carry

        lax.fori_loop(0, x1_ref.shape[0] // slab, body, 0)


def _ffn(h2, w1, w2, x1, mod4, g_final, S, tm=512, tf=1024, slab=16):
    T, D = x1.shape
    F = w1.shape[1]
    per_b = S // tm
    return pl.pallas_call(
        functools.partial(_ffn_kernel, slab=slab),
        out_shape=jax.ShapeDtypeStruct((T, D), F32),
        grid=(T // tm, F // tf),
        in_specs=[pl.BlockSpec((tm, D), lambda i, f: (i, 0)),
                  pl.BlockSpec((D, tf), lambda i, f: (0, f)),
                  pl.BlockSpec((tf, D), lambda i, f: (f, 0)),
                  pl.BlockSpec((tm, D), lambda i, f: (i, 0)),
                  pl.BlockSpec((None, None, 1, D), lambda i, f: (i // per_b, 5, 0, 0)),
                  pl.BlockSpec((1, D), lambda i, f: (0, 0))],
        out_specs=pl.BlockSpec((tm, D), lambda i, f: (i, 0)),
        scratch_shapes=[pltpu.VMEM((tm, D), F32)],
        compiler_params=_params(("parallel", "arbitrary")),
        name="ffn",
    )(h2, w1, w2, x1, mod4, g_final)


def _pad_to(a, rows, cols):
    return jnp.pad(a, ((0, rows - a.shape[0]), (0, cols - a.shape[1])))


def _pack_w_in(w, D):
    sizes = (D // 2, D // 2, D, D, GLA_RANK, D, D, D, FOX_HEADS, D, D)
    offs = [0]
    for s in sizes:
        offs.append(offs[-1] + s)
    seg = lambda i: w[:, offs[i]:offs[i + 1]]
    main = jnp.concatenate([seg(i) for i in (0, 1, 2, 3, 5, 6, 7, 9, 10)], axis=1).astype(BF16)
    low = jnp.concatenate([_pad_to(seg(4), D, LANES), _pad_to(seg(8), D, LANES)], axis=1).astype(BF16)
    return main, low


def _layer(x2d, cond_mod, l, B, S, D, w_ada, b_ada, g_norm1, w_in, w_gla_a_up, b_gla_a, g_gla_out,
           b_fox_f, w_proj_gla, w_proj_fox, w_out, g_norm2, w_ff1, w_ff2):
    del w_ada, b_ada
    mod4 = cond_mod.reshape(B, N_MOD, 1, D)
    w_main, w_low = _pack_w_in(w_in[l], D)
    proj, low = _in_proj(x2d, g_norm1[l][None, :], mod4, w_main, w_low, S)

    wup_pad = _pad_to(w_gla_a_up[l], LANES, D // 2).astype(BF16)
    o_a = _gla(proj, low, wup_pad, b_gla_a[l][None, :], g_gla_out[l][None, :], B, S, D)

    bf_pad = _pad_to(b_fox_f[l][None, :], 1, LANES)
    cum_col, cum_row = _fox_cum(low, bf_pad, B, S)
    o_b = _fox(proj, cum_col, cum_row, B, S, D)

    merged = _merge(o_a, o_b, w_proj_gla[l].astype(BF16), w_proj_fox[l].astype(BF16), proj, D)
    x1, h2 = _out_proj(merged, w_out[l].astype(BF16), x2d, mod4, g_norm2[l][None, :], S)
    return x1, h2, mod4


def kernel(x, c, w_ada, b_ada, g_norm1, w_in, w_gla_a_up, b_gla_a, g_gla_out, b_fox_f, w_proj_gla,
           w_proj_fox, w_out, g_norm2, w_ff1, w_ff2, g_final):
    B, S, D = x.shape
    depth = w_ada.shape[0]
    assert depth == 1, "the final norm is fused into the last layer's MLP kernel"
    x2d = x.reshape(B * S, D)
    l = 0
    mod = _ada(c, w_ada[l], b_ada[l][None, :])
    x1, h2, mod4 = _layer(x2d, mod, l, B, S, D, w_ada, b_ada, g_norm1, w_in, w_gla_a_up, b_gla_a,
                          g_gla_out, b_fox_f, w_proj_gla, w_proj_fox, w_out, g_norm2, w_ff1, w_ff2)
    out = _ffn(h2, w_ff1[l].astype(BF16), w_ff2[l].astype(BF16), x1, mod4, g_final[None, :], S)
    return out.reshape(B, S, D)
```

```python
import functools
import math

import jax
import jax.numpy as jnp
from jax import lax
from jax.experimental import pallas as pl
from jax.experimental.pallas import tpu as pltpu

F32 = jnp.float32
BF16 = jnp.bfloat16

EPS = 1e-6
LANES = 128
GLA_HEADS = 4
GLA_RANK = 16
GLA_TAU = 16.0
GLA_CHUNK = 64
FOX_HEADS = 16
N_MOD = 6
NEG_BIG = -1e30
VMEM_LIMIT = 56 * 1024 * 1024

_NT = (((1,), (1,)), ((), ()))
_TN = (((0,), (0,)), ((), ()))


def _dot(a, b):
    return jnp.dot(a, b, preferred_element_type=F32)


def _log_sigmoid(z):
    return jnp.minimum(z, 0.0) - jnp.log(1.0 + jnp.exp(-jnp.abs(z)))


def _silu(z):
    return z * jax.nn.sigmoid(z)


def _split_bf16(a):
    hi = a.astype(BF16)
    lo = (a - hi.astype(F32)).astype(BF16)
    return hi, lo


def _params(sem):
    return pltpu.CompilerParams(dimension_semantics=sem, vmem_limit_bytes=VMEM_LIMIT)


def _ada_kernel(c_ref, w_ref, b_ref, o_ref):
    cond = _silu(c_ref[...]).astype(BF16)
    o_ref[...] = _dot(cond, w_ref[...].astype(BF16)) + b_ref[...]


def _ada(c, w_ada, b_ada, tn=1024):
    B, D = c.shape
    N = w_ada.shape[1]
    return pl.pallas_call(
        _ada_kernel,
        out_shape=jax.ShapeDtypeStruct((B, N), F32),
        grid=(N // tn,),
        in_specs=[pl.BlockSpec((B, D), lambda j: (0, 0)),
                  pl.BlockSpec((D, tn), lambda j: (0, j)),
                  pl.BlockSpec((1, tn), lambda j: (0, j))],
        out_specs=pl.BlockSpec((B, tn), lambda j: (0, j)),
        compiler_params=_params(("arbitrary",)),
        name="ada",
    )(c, w_ada, b_ada)


def _modulated_norm_rows(x, g, scale, shift):
    y = x * lax.rsqrt(jnp.mean(x * x, axis=-1, keepdims=True) + EPS) * g
    return y * (1.0 + scale) + shift


def _in_kernel(x_ref, g_ref, scale_ref, shift_ref, w_ref, wlow_ref, proj_ref, low_ref, h_ref, *, slab):
    @pl.when(pl.program_id(1) == 0)
    def _():
        g = g_ref[...]
        scale = scale_ref[...]
        shift = shift_ref[...]

        def body(s, carry):
            r = pl.ds(pl.multiple_of(s * slab, slab), slab)
            h_ref[r, :] = _modulated_norm_rows(x_ref[r, :], g, scale, shift).astype(BF16)
            return carry

        lax.fori_loop(0, x_ref.shape[0] // slab, body, 0)
        low_ref[...] = _dot(h_ref[...], wlow_ref[...])

    proj_ref[...] = _dot(h_ref[...], w_ref[...]).astype(proj_ref.dtype)


def _in_proj(x2d, g1, mod4, w_main, w_low, S, tm=1024, tn=1024, slab=16):
    T, D = x2d.shape
    N = w_main.shape[1]
    NL = w_low.shape[1]
    per_b = S // tm
    return pl.pallas_call(
        functools.partial(_in_kernel, slab=slab),
        out_shape=(jax.ShapeDtypeStruct((T, N), BF16), jax.ShapeDtypeStruct((T, NL), F32)),
        grid=(T // tm, N // tn),
        in_specs=[pl.BlockSpec((tm, D), lambda i, j: (i, 0)),
                  pl.BlockSpec((1, D), lambda i, j: (0, 0)),
                  pl.BlockSpec((None, None, 1, D), lambda i, j: (i // per_b, 1, 0, 0)),
                  pl.BlockSpec((None, None, 1, D), lambda i, j: (i // per_b, 0, 0, 0)),
                  pl.BlockSpec((D, tn), lambda i, j: (0, j)),
                  pl.BlockSpec((D, NL), lambda i, j: (0, 0))],
        out_specs=(pl.BlockSpec((tm, tn), lambda i, j: (i, j)),
                   pl.BlockSpec((tm, NL), lambda i, j: (i, 0))),
        scratch_shapes=[pltpu.VMEM((tm, D), BF16)],
        compiler_params=_params(("parallel", "arbitrary")),
        name="in_proj",
    )(x2d, g1, mod4, mod4, w_main, w_low)


def _gla_kernel(q_ref, k_ref, v_ref, gg_ref, low_ref, wup_ref, ba_ref, g_ref, o_ref, *, q_scale, chunk):
    C, SB = chunk, GLA_CHUNK
    S, dk = q_ref.shape
    dv = v_ref.shape[1]
    tril = (lax.broadcasted_iota(jnp.int32, (C, C), 1) <= lax.broadcasted_iota(jnp.int32, (C, C), 0)).astype(BF16)
    wup = wup_ref[...]
    ba = ba_ref[...]
    g = g_ref[...]
    state_t = jnp.zeros((dv, dk), F32)
    for c in range(S // C):
        rows = slice(c * C, (c + 1) * C)
        q = q_ref[rows, :].astype(F32) * q_scale
        k = k_ref[rows, :].astype(F32)
        v = v_ref[rows, :]
        z = _dot(low_ref[rows, :].astype(BF16), wup) + ba
        la = _log_sigmoid(z) * (1.0 / GLA_TAU)
        la_hi, la_lo = _split_bf16(la)
        b = _dot(tril, la_hi) + _dot(tril, la_lo)
        b_last = b[C - 1:C, :]
        q_dec = (q * jnp.exp(b)).astype(BF16)
        k_dec = (k * jnp.exp(b_last - b)).astype(BF16)

        attn_rows = []
        for i in range(C // SB):
            sub = slice(i * SB, (i + 1) * SB)
            n_keys = -(-(i + 1) * SB // LANES) * LANES
            ref = b[i * SB - 1:i * SB, :] if i > 0 else jnp.zeros((1, dk), F32)
            q_loc = (q[sub] * jnp.exp(b[sub] - ref)).astype(BF16)
            k_loc = (k[:n_keys] * jnp.exp(ref - b[:n_keys])).astype(BF16)
            a = lax.dot_general(q_loc, k_loc, _NT, preferred_element_type=F32)
            key_i = lax.broadcasted_iota(jnp.int32, (SB, n_keys), 1)
            qry_i = lax.broadcasted_iota(jnp.int32, (SB, n_keys), 0) + i * SB
            a = jnp.where(key_i <= qry_i, a, 0.0)
            if n_keys < C:
                a = jnp.concatenate([a, jnp.zeros((SB, C - n_keys), F32)], axis=1)
            attn_rows.append(a)
        attn = jnp.concatenate(attn_rows, axis=0).astype(BF16)

        o = _dot(attn, v) + lax.dot_general(q_dec, state_t.astype(BF16), _NT, preferred_element_type=F32)
        state_t = state_t * jnp.exp(b_last) + lax.dot_general(v, k_dec, _TN, preferred_element_type=F32)
        y = o * lax.rsqrt(jnp.mean(o * o, axis=-1, keepdims=True) + EPS) * g
        o_ref[rows, :] = (y * _silu(gg_ref[rows, :].astype(F32))).astype(o_ref.dtype)


def _gla(proj, low, wup_pad, b_a, g_out, B, S, D, chunk=256):
    T = B * S
    H = GLA_HEADS
    dk = (D // 2) // H
    dv = D // H
    k_blk0 = (D // 2) // dk
    v_blk0 = D // dv
    gg_blk0 = 2 * D // dv
    return pl.pallas_call(
        functools.partial(_gla_kernel, q_scale=dk ** -0.5, chunk=chunk),
        out_shape=jax.ShapeDtypeStruct((T, D), BF16),
        grid=(B, H),
        in_specs=[pl.BlockSpec((S, dk), lambda b, h: (b, h)),
                  pl.BlockSpec((S, dk), lambda b, h: (b, k_blk0 + h)),
                  pl.BlockSpec((S, dv), lambda b, h: (b, v_blk0 + h)),
                  pl.BlockSpec((S, dv), lambda b, h: (b, gg_blk0 + h)),
                  pl.BlockSpec((S, LANES), lambda b, h: (b, 0)),
                  pl.BlockSpec((LANES, dk), lambda b, h: (0, h)),
                  pl.BlockSpec((1, dk), lambda b, h: (0, h)),
                  pl.BlockSpec((1, dv), lambda b, h: (0, 0))],
        out_specs=pl.BlockSpec((S, dv), lambda b, h: (b, h)),
        compiler_params=_params(("parallel", "arbitrary")),
        name="gla",
    )(proj, proj, proj, proj, low, wup_pad, b_a, g_out)


def _cum_kernel(low_ref, bf_ref, col_ref, row_ref, *, blk):
    S = low_ref.shape[0]
    n_heads = row_ref.shape[0]
    r_i = lax.broadcasted_iota(jnp.int32, (blk, blk), 0)
    c_i = lax.broadcasted_iota(jnp.int32, (blk, blk), 1)
    tril = (c_i <= r_i).astype(BF16)
    triu = (r_i <= c_i).astype(BF16)
    bf = bf_ref[...]
    carry_row = jnp.zeros((1, LANES), F32)
    carry_col = jnp.zeros((LANES, 1), F32)
    for i in range(S // blk):
        r = slice(i * blk, (i + 1) * blk)
        lf = _log_sigmoid(low_ref[r, :] + bf)
        hi, lo = _split_bf16(lf)
        loc = _dot(tril, hi) + _dot(tril, lo) + carry_row
        col_ref[r, :] = loc
        carry_row = loc[blk - 1:blk, :]
        loc_t = (lax.dot_general(hi, triu, _TN, preferred_element_type=F32)
                 + lax.dot_general(lo, triu, _TN, preferred_element_type=F32) + carry_col)
        row_ref[:, r] = loc_t[:n_heads, :]
        carry_col = loc_t[:, blk - 1:blk]


def _fox_cum(low, bf_pad, B, S, blk=128):
    return pl.pallas_call(
        functools.partial(_cum_kernel, blk=blk),
        out_shape=(jax.ShapeDtypeStruct((B, S, LANES), F32),
                   jax.ShapeDtypeStruct((B, FOX_HEADS, S), F32)),
        grid=(B,),
        in_specs=[pl.BlockSpec((S, LANES), lambda b: (b, 1)),
                  pl.BlockSpec((1, LANES), lambda b: (0, 0))],
        out_specs=(pl.BlockSpec((None, S, LANES), lambda b: (b, 0, 0)),
                   pl.BlockSpec((None, FOX_HEADS, S), lambda b: (b, 0, 0))),
        compiler_params=_params(("arbitrary",)),
        name="fox_cum",
    )(low, bf_pad)


def _fox_kernel(q_ref, k_ref, v_ref, cc_ref, cr_ref, o_ref, m_ref, l_ref, acc_ref, *, blk, scale):
    S, hd = q_ref.shape
    h = pl.program_id(1)
    lane = lax.broadcasted_iota(jnp.int32, (S, LANES), 1)
    ck_all = jnp.sum(jnp.where(lane == h, cc_ref[...], 0.0), axis=-1, keepdims=True)
    cq_all = cr_ref[pl.ds(h, 1), :]
    key_i = lax.broadcasted_iota(jnp.int32, (blk, blk), 0)
    qry_i = lax.broadcasted_iota(jnp.int32, (blk, blk), 1)
    causal = key_i <= qry_i

    for j in range(S // blk):
        r0 = j * blk
        ks = slice(r0, r0 + blk)
        t = lax.dot_general(k_ref[ks, :], q_ref[r0:, :], _NT, preferred_element_type=F32) * scale
        t = t - ck_all[ks]
        t_diag = jnp.where(causal, t[:, :blk], NEG_BIG)
        t = t_diag if r0 + blk == S else jnp.concatenate([t_diag, t[:, blk:]], axis=1)
        cq = cq_all[:, r0:]
        t_max = jnp.max(t, axis=0, keepdims=True) + cq
        if j == 0:
            m_new = t_max
        else:
            m_old = m_ref[:, r0:]
            m_new = jnp.maximum(m_old, t_max)
            alpha = jnp.exp(m_old - m_new)
        p = jnp.exp(t - (m_new - cq))
        p_sum = jnp.sum(p, axis=0, keepdims=True)
        pv = lax.dot_general(v_ref[ks, :], p.astype(BF16), _TN, preferred_element_type=F32)
        if j == 0:
            l_ref[...] = p_sum
            acc_ref[...] = pv
        else:
            l_ref[:, r0:] = alpha * l_ref[:, r0:] + p_sum
            acc_ref[:, r0:] = alpha * acc_ref[:, r0:] + pv
        m_ref[:, r0:] = m_new
    o_ref[...] = (acc_ref[...] / l_ref[...]).T.astype(o_ref.dtype)


def _fox(proj, cum_col, cum_row, B, S, D, blk=256):
    T = B * S
    H = FOX_HEADS
    hd = D // H
    q_blk0, k_blk0, v_blk0 = 3 * D // hd, 4 * D // hd, 5 * D // hd
    return pl.pallas_call(
        functools.partial(_fox_kernel, blk=blk, scale=1.0 / math.sqrt(hd)),
        out_shape=jax.ShapeDtypeStruct((T, D), BF16),
        grid=(B, H),
        in_specs=[pl.BlockSpec((S, hd), lambda b, h: (b, q_blk0 + h)),
                  pl.BlockSpec((S, hd), lambda b, h: (b, k_blk0 + h)),
                  pl.BlockSpec((S, hd), lambda b, h: (b, v_blk0 + h)),
                  pl.BlockSpec((None, S, LANES), lambda b, h: (b, 0, 0)),
                  pl.BlockSpec((None, H, S), lambda b, h: (b, 0, 0))],
        out_specs=pl.BlockSpec((S, hd), lambda b, h: (b, h)),
        scratch_shapes=[pltpu.VMEM((1, S), F32), pltpu.VMEM((1, S), F32), pltpu.VMEM((hd, S), F32)],
        compiler_params=_params(("parallel", "arbitrary")),
        name="fox",
    )(proj, proj, proj, cum_col, cum_row)


def _merge_kernel(oa_ref, ob_ref, wa_ref, wb_ref, ga_ref, gb_ref, o_ref):
    a = _dot(oa_ref[...], wa_ref[...])
    b = _dot(ob_ref[...], wb_ref[...])
    merged = jax.nn.sigmoid(ga_ref[...].astype(F32)) * a + jax.nn.sigmoid(gb_ref[...].astype(F32)) * b
    o_ref[...] = merged.astype(o_ref.dtype)


def _merge(o_a, o_b, w_a, w_b, proj, D, tm=512, tn=1024):
    T = o_a.shape[0]
    ga_blk0, gb_blk0 = 6 * D // tn, 7 * D // tn
    return pl.pallas_call(
        _merge_kernel,
        out_shape=jax.ShapeDtypeStruct((T, D), BF16),
        grid=(D // tn, T // tm),
        in_specs=[pl.BlockSpec((tm, D), lambda j, i: (i, 0)),
                  pl.BlockSpec((tm, D), lambda j, i: (i, 0)),
                  pl.BlockSpec((D, tn), lambda j, i: (0, j)),
                  pl.BlockSpec((D, tn), lambda j, i: (0, j)),
                  pl.BlockSpec((tm, tn), lambda j, i: (i, ga_blk0 + j)),
                  pl.BlockSpec((tm, tn), lambda j, i: (i, gb_blk0 + j))],
        out_specs=pl.BlockSpec((tm, tn), lambda j, i: (i, j)),
        compiler_params=_params(("arbitrary", "parallel")),
        name="merge",
    )(o_a, o_b, w_a, w_b, proj, proj)


def _out_kernel(m_ref, w_ref, x_ref, gate_ref, g_ref, scale_ref, shift_ref, x1_ref, h2_ref, *, slab):
    x1_ref[...] = _dot(m_ref[...], w_ref[...])
    gate = gate_ref[...]
    g = g_ref[...]
    scale = scale_ref[...]
    shift = shift_ref[...]

    def body(s, carry):
        r = pl.ds(pl.multiple_of(s * slab, slab), slab)
        x1 = x_ref[r, :] + gate * x1_ref[r, :]
        x1_ref[r, :] = x1
        h2_ref[r, :] = _modulated_norm_rows(x1, g, scale, shift).astype(BF16)
        return carry

    lax.fori_loop(0, x_ref.shape[0] // slab, body, 0)


def _out_proj(merged, w_out, x2d, mod4, g2, S, tm=512, slab=16):
    T, D = x2d.shape
    per_b = S // tm
    mod_spec = lambda k: pl.BlockSpec((None, None, 1, D), lambda i: (i // per_b, k, 0, 0))
    return pl.pallas_call(
        functools.partial(_out_kernel, slab=slab),
        out_shape=(jax.ShapeDtypeStruct((T, D), F32), jax.ShapeDtypeStruct((T, D), BF16)),
        grid=(T // tm,),
        in_specs=[pl.BlockSpec((tm, D), lambda i: (i, 0)),
                  pl.BlockSpec((D, D), lambda i: (0, 0)),
                  pl.BlockSpec((tm, D), lambda i: (i, 0)),
                  mod_spec(2),
                  pl.BlockSpec((1, D), lambda i: (0, 0)),
                  mod_spec(4),
                  mod_spec(3)],
        out_specs=(pl.BlockSpec((tm, D), lambda i: (i, 0)),
                   pl.BlockSpec((tm, D), lambda i: (i, 0))),
        compiler_params=_params(("parallel",)),
        name="out_proj",
    )(merged, w_out, x2d, mod4, g2, mod4, mod4)


def _ffn_kernel(h_ref, w1_ref, w2_ref, x1_ref, gate_ref, g_ref, o_ref, acc_ref, *, slab):
    f = pl.program_id(1)

    @pl.when(f == 0)
    def _():
        acc_ref[...] = jnp.zeros_like(acc_ref)

    a = jnp.maximum(_dot(h_ref[...], w1_ref[...]), 0.0)
    acc_ref[...] += _dot((a * a).astype(BF16), w2_ref[...])

    @pl.when(f == pl.num_programs(1) - 1)
    def _():
        gate = gate_ref[...]
        g = g_ref[...]

        def body(s, carry):
            r = pl.ds(pl.multiple_of(s * slab, slab), slab)
            x2 = x1_ref[r, :] + gate * acc_ref[r, :]
            o_ref[r, :] = x2 * lax.rsqrt(jnp.mean(x2 * x2, axis=-1, keepdims=True) + EPS) * g
            return carry

        lax.fori_loop(0, x1_ref.shape[0] // slab, body, 0)


def _ffn(h2, w1, w2, x1, mod4, g_final, S, tm=512, tf=1024, slab=16):
    T, D = x1.shape
    F = w1.shape[1]
    per_b = S // tm
    return pl.pallas_call(
        functools.partial(_ffn_kernel, slab=slab),
        out_shape=jax.ShapeDtypeStruct((T, D), F32),
        grid=(T // tm, F // tf),
        in_specs=[pl.BlockSpec((tm, D), lambda i, f: (i, 0)),
                  pl.BlockSpec((D, tf), lambda i, f: (0, f)),
                  pl.BlockSpec((tf, D), lambda i, f: (f, 0)),
                  pl.BlockSpec((tm, D), lambda i, f: (i, 0)),
                  pl.BlockSpec((None, None, 1, D), lambda i, f: (i // per_b, 5, 0, 0)),
                  pl.BlockSpec((1, D), lambda i, f: (0, 0))],
        out_specs=pl.BlockSpec((tm, D), lambda i, f: (i, 0)),
        scratch_shapes=[pltpu.VMEM((tm, D), F32)],
        compiler_params=_params(("parallel", "arbitrary")),
        name="ffn",
    )(h2, w1, w2, x1, mod4, g_final)


def _pad_to(a, rows, cols):
    return jnp.pad(a, ((0, rows - a.shape[0]), (0, cols - a.shape[1])))


def _pack_w_in(w, D):
    sizes = (D // 2, D // 2, D, D, GLA_RANK, D, D, D, FOX_HEADS, D, D)
    offs = [0]
    for s in sizes:
        offs.append(offs[-1] + s)
    seg = lambda i: w[:, offs[i]:offs[i + 1]]
    main = jnp.concatenate([seg(i) for i in (0, 1, 2, 3, 5, 6, 7, 9, 10)], axis=1).astype(BF16)
    low = jnp.concatenate([_pad_to(seg(4), D, LANES), _pad_to(seg(8), D, LANES)], axis=1).astype(BF16)
    return main, low


def _layer(x2d, cond_mod, l, B, S, D, w_ada, b_ada, g_norm1, w_in, w_gla_a_up, b_gla_a, g_gla_out,
           b_fox_f, w_proj_gla, w_proj_fox, w_out, g_norm2, w_ff1, w_ff2):
    del w_ada, b_ada
    mod4 = cond_mod.reshape(B, N_MOD, 1, D)
    w_main, w_low = _pack_w_in(w_in[l], D)
    proj, low = _in_proj(x2d, g_norm1[l][None, :], mod4, w_main, w_low, S)

    wup_pad = _pad_to(w_gla_a_up[l], LANES, D // 2).astype(BF16)
    o_a = _gla(proj, low, wup_pad, b_gla_a[l][None, :], g_gla_out[l][None, :], B, S, D)

    bf_pad = _pad_to(b_fox_f[l][None, :], 1, LANES)
    cum_col, cum_row = _fox_cum(low, bf_pad, B, S)
    o_b = _fox(proj, cum_col, cum_row, B, S, D)

    merged = _merge(o_a, o_b, w_proj_gla[l].astype(BF16), w_proj_fox[l].astype(BF16), proj, D)
    x1, h2 = _out_proj(merged, w_out[l].astype(BF16), x2d, mod4, g_norm2[l][None, :], S)
    return x1, h2, mod4


def kernel(x, c, w_ada, b_ada, g_norm1, w_in, w_gla_a_up, b_gla_a, g_gla_out, b_fox_f, w_proj_gla,
           w_proj_fox, w_out, g_norm2, w_ff1, w_ff2, g_final):
    B, S, D = x.shape
    depth = w_ada.shape[0]
    assert depth == 1, "the final norm is fused into the last layer's MLP kernel"
    x2d = x.reshape(B * S, D)
    l = 0
    mod = _ada(c, w_ada[l], b_ada[l][None, :])
    x1, h2, mod4 = _layer(x2d, mod, l, B, S, D, w_ada, b_ada, g_norm1, w_in, w_gla_a_up, b_gla_a,
                          g_gla_out, b_fox_f, w_proj_gla, w_proj_fox, w_out, g_norm2, w_ff1, w_ff2)
    out = _ffn(h2, w_ff1[l].astype(BF16), w_ff2[l].astype(BF16), x1, mod4, g_final[None, :], S)
    return out.reshape(B, S, D)
```

```python
import functools
import math

import jax
import jax.numpy as jnp
from jax import lax
from jax.experimental import pallas as pl
from jax.experimental.pallas import tpu as pltpu

F32 = jnp.float32
BF16 = jnp.bfloat16

EPS = 1e-6
LANES = 128
GLA_HEADS = 4
GLA_RANK = 16
GLA_TAU = 16.0
GLA_CHUNK = 64
FOX_HEADS = 16
N_MOD = 6
NEG_BIG = -1e30
LOG2E = math.log2(math.e)
SLAB = 16
SLAB_UNROLL = 4
VMEM_LIMIT = 56 * 1024 * 1024

_NT = (((1,), (1,)), ((), ()))
_TN = (((0,), (0,)), ((), ()))


def _dot(a, b):
    return jnp.dot(a, b, preferred_element_type=F32)


def _log_sigmoid(z):
    return jnp.minimum(z, 0.0) - jnp.log(1.0 + jnp.exp(-jnp.abs(z)))


def _silu(z):
    return z * jax.nn.sigmoid(z)


def _split_bf16(a):
    hi = a.astype(BF16)
    lo = (a - hi.astype(F32)).astype(BF16)
    return hi, lo


def _params(sem):
    return pltpu.CompilerParams(dimension_semantics=sem, vmem_limit_bytes=VMEM_LIMIT)


def _ada_kernel(c_ref, w_ref, b_ref, o_ref):
    cond = _silu(c_ref[...]).astype(BF16)
    o_ref[...] = _dot(cond, w_ref[...].astype(BF16)) + b_ref[...]


def _ada(c, w_ada, b_ada, tn=1024):
    B, D = c.shape
    N = w_ada.shape[1]
    return pl.pallas_call(
        _ada_kernel,
        out_shape=jax.ShapeDtypeStruct((B, N), F32),
        grid=(N // tn,),
        in_specs=[pl.BlockSpec((B, D), lambda j: (0, 0)),
                  pl.BlockSpec((D, tn), lambda j: (0, j)),
                  pl.BlockSpec((1, tn), lambda j: (0, j))],
        out_specs=pl.BlockSpec((B, tn), lambda j: (0, j)),
        compiler_params=_params(("arbitrary",)),
        name="ada",
    )(c, w_ada, b_ada)


def _modulated_norm_rows(x, gain, shift):
    return x * lax.rsqrt(jnp.mean(x * x, axis=-1, keepdims=True) + EPS) * gain + shift


def _for_row_slabs(n_rows, body):
    def step(s, carry):
        body(pl.ds(pl.multiple_of(s * SLAB, SLAB), SLAB))
        return carry

    lax.fori_loop(0, n_rows // SLAB, step, 0, unroll=SLAB_UNROLL)


def _in_kernel(x_ref, g_ref, scale_ref, shift_ref, w_ref, wlow_ref, proj_ref, low_ref, h_ref):
    @pl.when(pl.program_id(1) == 0)
    def _():
        gain = g_ref[...] * (1.0 + scale_ref[...])
        shift = shift_ref[...]

        def body(r):
            h_ref[r, :] = _modulated_norm_rows(x_ref[r, :], gain, shift).astype(BF16)

        _for_row_slabs(x_ref.shape[0], body)
        low_ref[...] = _dot(h_ref[...], wlow_ref[...])

    proj_ref[...] = _dot(h_ref[...], w_ref[...]).astype(proj_ref.dtype)


def _in_proj(x2d, g1, mod4, w_main, w_low, S, tm=1024, tn=1024):
    T, D = x2d.shape
    N = w_main.shape[1]
    NL = w_low.shape[1]
    per_b = S // tm
    return pl.pallas_call(
        _in_kernel,
        out_shape=(jax.ShapeDtypeStruct((T, N), BF16), jax.ShapeDtypeStruct((T, NL), F32)),
        grid=(T // tm, N // tn),
        in_specs=[pl.BlockSpec((tm, D), lambda i, j: (i, 0)),
                  pl.BlockSpec((1, D), lambda i, j: (0, 0)),
                  pl.BlockSpec((None, None, 1, D), lambda i, j: (i // per_b, 1, 0, 0)),
                  pl.BlockSpec((None, None, 1, D), lambda i, j: (i // per_b, 0, 0, 0)),
                  pl.BlockSpec((D, tn), lambda i, j: (0, j)),
                  pl.BlockSpec((D, NL), lambda i, j: (0, 0))],
        out_specs=(pl.BlockSpec((tm, tn), lambda i, j: (i, j)),
                   pl.BlockSpec((tm, NL), lambda i, j: (i, 0))),
        scratch_shapes=[pltpu.VMEM((tm, D), BF16)],
        compiler_params=_params(("parallel", "arbitrary")),
        name="in_proj",
    )(x2d, g1, mod4, mod4, w_main, w_low)


def _gla_kernel(q_ref, k_ref, v_ref, gg_ref, low_ref, wup_ref, ba_ref, g_ref, o_ref, *, q_scale, chunk):
    C, SB = chunk, GLA_CHUNK
    S, dk = q_ref.shape
    dv = v_ref.shape[1]
    tril = (lax.broadcasted_iota(jnp.int32, (C, C), 1) <= lax.broadcasted_iota(jnp.int32, (C, C), 0)).astype(BF16)
    wup = wup_ref[...]
    ba = ba_ref[...]
    g = g_ref[...]
    state_t = jnp.zeros((dv, dk), F32)
    for c in range(S // C):
        rows = slice(c * C, (c + 1) * C)
        q = q_ref[rows, :].astype(F32) * q_scale
        k = k_ref[rows, :].astype(F32)
        v = v_ref[rows, :]
        z = _dot(low_ref[rows, :].astype(BF16), wup) + ba
        la = _log_sigmoid(z) * (1.0 / GLA_TAU)
        la_hi, la_lo = _split_bf16(la)
        b = _dot(tril, la_hi) + _dot(tril, la_lo)
        b_last = b[C - 1:C, :]
        q_dec = (q * jnp.exp(b)).astype(BF16)
        k_dec = (k * jnp.exp(b_last - b)).astype(BF16)

        attn_rows = []
        for i in range(C // SB):
            sub = slice(i * SB, (i + 1) * SB)
            n_keys = -(-(i + 1) * SB // LANES) * LANES
            ref = b[i * SB - 1:i * SB, :] if i > 0 else jnp.zeros((1, dk), F32)
            q_loc = (q[sub] * jnp.exp(b[sub] - ref)).astype(BF16)
            k_loc = (k[:n_keys] * jnp.exp(ref - b[:n_keys])).astype(BF16)
            a = lax.dot_general(q_loc, k_loc, _NT, preferred_element_type=F32)
            key_i = lax.broadcasted_iota(jnp.int32, (SB, n_keys), 1)
            qry_i = lax.broadcasted_iota(jnp.int32, (SB, n_keys), 0) + i * SB
            a = jnp.where(key_i <= qry_i, a, 0.0)
            if n_keys < C:
                a = jnp.concatenate([a, jnp.zeros((SB, C - n_keys), F32)], axis=1)
            attn_rows.append(a)
        attn = jnp.concatenate(attn_rows, axis=0).astype(BF16)

        o = _dot(attn, v) + lax.dot_general(q_dec, state_t.astype(BF16), _NT, preferred_element_type=F32)
        state_t = state_t * jnp.exp(b_last) + lax.dot_general(v, k_dec, _TN, preferred_element_type=F32)
        y = o * lax.rsqrt(jnp.mean(o * o, axis=-1, keepdims=True) + EPS) * g
        o_ref[rows, :] = (y * _silu(gg_ref[rows, :].astype(F32))).astype(o_ref.dtype)


def _gla(proj, low, wup_pad, b_a, g_out, B, S, D, chunk=256):
    T = B * S
    H = GLA_HEADS
    dk = (D // 2) // H
    dv = D // H
    k_blk0 = (D // 2) // dk
    v_blk0 = D // dv
    gg_blk0 = 2 * D // dv
    return pl.pallas_call(
        functools.partial(_gla_kernel, q_scale=dk ** -0.5, chunk=chunk),
        out_shape=jax.ShapeDtypeStruct((T, D), BF16),
        grid=(B, H),
        in_specs=[pl.BlockSpec((S, dk), lambda b, h: (b, h)),
                  pl.BlockSpec((S, dk), lambda b, h: (b, k_blk0 + h)),
                  pl.BlockSpec((S, dv), lambda b, h: (b, v_blk0 + h)),
                  pl.BlockSpec((S, dv), lambda b, h: (b, gg_blk0 + h)),
                  pl.BlockSpec((S, LANES), lambda b, h: (b, 0)),
                  pl.BlockSpec((LANES, dk), lambda b, h: (0, h)),
                  pl.BlockSpec((1, dk), lambda b, h: (0, h)),
                  pl.BlockSpec((1, dv), lambda b, h: (0, 0))],
        out_specs=pl.BlockSpec((S, dv), lambda b, h: (b, h)),
        compiler_params=_params(("parallel", "arbitrary")),
        name="gla",
    )(proj, proj, proj, proj, low, wup_pad, b_a, g_out)


def _cum_kernel(low_ref, bf_ref, col_ref, row_ref, *, blk):
    S = low_ref.shape[0]
    n_heads = row_ref.shape[0]
    r_i = lax.broadcasted_iota(jnp.int32, (blk, blk), 0)
    c_i = lax.broadcasted_iota(jnp.int32, (blk, blk), 1)
    tril = (c_i <= r_i).astype(BF16)
    triu = (r_i <= c_i).astype(BF16)
    bf = bf_ref[...]
    carry_row = jnp.zeros((1, LANES), F32)
    carry_col = jnp.zeros((LANES, 1), F32)
    for i in range(S // blk):
        r = slice(i * blk, (i + 1) * blk)
        lf = _log_sigmoid(low_ref[r, :] + bf)
        hi, lo = _split_bf16(lf)
        loc = _dot(tril, hi) + _dot(tril, lo) + carry_row
        col_ref[r, :] = loc
        carry_row = loc[blk - 1:blk, :]
        loc_t = (lax.dot_general(hi, triu, _TN, preferred_element_type=F32)
                 + lax.dot_general(lo, triu, _TN, preferred_element_type=F32) + carry_col)
        row_ref[:, r] = loc_t[:n_heads, :]
        carry_col = loc_t[:, blk - 1:blk]


def _fox_cum(low, bf_pad, B, S, blk=128):
    return pl.pallas_call(
        functools.partial(_cum_kernel, blk=blk),
        out_shape=(jax.ShapeDtypeStruct((B, S, LANES), F32),
                   jax.ShapeDtypeStruct((B, FOX_HEADS, S), F32)),
        grid=(B,),
        in_specs=[pl.BlockSpec((S, LANES), lambda b: (b, 1)),
                  pl.BlockSpec((1, LANES), lambda b: (0, 0))],
        out_specs=(pl.BlockSpec((None, S, LANES), lambda b: (b, 0, 0)),
                   pl.BlockSpec((None, FOX_HEADS, S), lambda b: (b, 0, 0))),
        compiler_params=_params(("arbitrary",)),
        name="fox_cum",
    )(low, bf_pad)


def _fox_kernel(q_ref, k_ref, v_ref, cc_ref, cr_ref, o_ref, qs_ref, m_ref, l_ref, acc_ref, *, blk, scale):
    S, hd = q_ref.shape
    h = pl.program_id(1)
    qs_ref[...] = (q_ref[...].astype(F32) * (scale * LOG2E)).astype(BF16)
    lane = lax.broadcasted_iota(jnp.int32, (S, LANES), 1)
    ck_all = jnp.sum(jnp.where(lane == h, cc_ref[...], 0.0), axis=-1, keepdims=True) * LOG2E
    cq_all = cr_ref[pl.ds(h, 1), :] * LOG2E
    key_i = lax.broadcasted_iota(jnp.int32, (blk, blk), 0)
    qry_i = lax.broadcasted_iota(jnp.int32, (blk, blk), 1)
    causal = key_i <= qry_i

    for j in range(S // blk):
        r0 = j * blk
        ks = slice(r0, r0 + blk)
        t = lax.dot_general(k_ref[ks, :], qs_ref[r0:, :], _NT, preferred_element_type=F32) - ck_all[ks]
        t_diag = jnp.where(causal, t[:, :blk], NEG_BIG)
        t = t_diag if r0 + blk == S else jnp.concatenate([t_diag, t[:, blk:]], axis=1)
        cq = cq_all[:, r0:]
        t_max = jnp.max(t, axis=0, keepdims=True) + cq
        if j == 0:
            m_new = t_max
        else:
            m_old = m_ref[:, r0:]
            m_new = jnp.maximum(m_old, t_max)
            alpha = jnp.exp2(m_old - m_new)
        p = jnp.exp2(t - (m_new - cq))
        p_sum = jnp.sum(p, axis=0, keepdims=True)
        pv = lax.dot_general(v_ref[ks, :], p.astype(BF16), _TN, preferred_element_type=F32)
        if j == 0:
            l_ref[...] = p_sum
            acc_ref[...] = pv
        else:
            l_ref[:, r0:] = alpha * l_ref[:, r0:] + p_sum
            acc_ref[:, r0:] = alpha * acc_ref[:, r0:] + pv
        m_ref[:, r0:] = m_new
    o_ref[...] = (acc_ref[...] / l_ref[...]).T.astype(o_ref.dtype)


def _fox(proj, cum_col, cum_row, B, S, D, blk=256):
    T = B * S
    H = FOX_HEADS
    hd = D // H
    q_blk0, k_blk0, v_blk0 = 3 * D // hd, 4 * D // hd, 5 * D // hd
    return pl.pallas_call(
        functools.partial(_fox_kernel, blk=blk, scale=1.0 / math.sqrt(hd)),
        out_shape=jax.ShapeDtypeStruct((T, D), BF16),
        grid=(B, H),
        in_specs=[pl.BlockSpec((S, hd), lambda b, h: (b, q_blk0 + h)),
                  pl.BlockSpec((S, hd), lambda b, h: (b, k_blk0 + h)),
                  pl.BlockSpec((S, hd), lambda b, h: (b, v_blk0 + h)),
                  pl.BlockSpec((None, S, LANES), lambda b, h: (b, 0, 0)),
                  pl.BlockSpec((None, H, S), lambda b, h: (b, 0, 0))],
        out_specs=pl.BlockSpec((S, hd), lambda b, h: (b, h)),
        scratch_shapes=[pltpu.VMEM((S, hd), BF16), pltpu.VMEM((1, S), F32), pltpu.VMEM((1, S), F32),
                        pltpu.VMEM((hd, S), F32)],
        compiler_params=_params(("parallel", "arbitrary")),
        name="fox",
    )(proj, proj, proj, cum_col, cum_row)


def _merge_kernel(oa_ref, ob_ref, wa_ref, wb_ref, ga_ref, gb_ref, o_ref):
    a = _dot(oa_ref[...], wa_ref[...])
    b = _dot(ob_ref[...], wb_ref[...])
    merged = jax.nn.sigmoid(ga_ref[...].astype(F32)) * a + jax.nn.sigmoid(gb_ref[...].astype(F32)) * b
    o_ref[...] = merged.astype(o_ref.dtype)


def _merge(o_a, o_b, w_a, w_b, proj, D, tm=512, tn=1024):
    T = o_a.shape[0]
    ga_blk0, gb_blk0 = 6 * D // tn, 7 * D // tn
    return pl.pallas_call(
        _merge_kernel,
        out_shape=jax.ShapeDtypeStruct((T, D), BF16),
        grid=(D // tn, T // tm),
        in_specs=[pl.BlockSpec((tm, D), lambda j, i: (i, 0)),
                  pl.BlockSpec((tm, D), lambda j, i: (i, 0)),
                  pl.BlockSpec((D, tn), lambda j, i: (0, j)),
                  pl.BlockSpec((D, tn), lambda j, i: (0, j)),
                  pl.BlockSpec((tm, tn), lambda j, i: (i, ga_blk0 + j)),
                  pl.BlockSpec((tm, tn), lambda j, i: (i, gb_blk0 + j))],
        out_specs=pl.BlockSpec((tm, tn), lambda j, i: (i, j)),
        compiler_params=_params(("arbitrary", "parallel")),
        name="merge",
    )(o_a, o_b, w_a, w_b, proj, proj)


def _out_kernel(m_ref, w_ref, x_ref, gate_ref, g_ref, scale_ref, shift_ref, x1_ref, h2_ref):
    x1_ref[...] = _dot(m_ref[...], w_ref[...])
    gate = gate_ref[...]
    gain = g_ref[...] * (1.0 + scale_ref[...])
    shift = shift_ref[...]

    def body(r):
        x1 = x_ref[r, :] + gate * x1_ref[r, :]
        x1_ref[r, :] = x1
        h2_ref[r, :] = _modulated_norm_rows(x1, gain, shift).astype(BF16)

    _for_row_slabs(x_ref.shape[0], body)


def _out_proj(merged, w_out, x2d, mod4, g2, S, tm=512):
    T, D = x2d.shape
    per_b = S // tm
    mod_spec = lambda k: pl.BlockSpec((None, None, 1, D), lambda i: (i // per_b, k, 0, 0))
    return pl.pallas_call(
        _out_kernel,
        out_shape=(jax.ShapeDtypeStruct((T, D), F32), jax.ShapeDtypeStruct((T, D), BF16)),
        grid=(T // tm,),
        in_specs=[pl.BlockSpec((tm, D), lambda i: (i, 0)),
                  pl.BlockSpec((D, D), lambda i: (0, 0)),
                  pl.BlockSpec((tm, D), lambda i: (i, 0)),
                  mod_spec(2),
                  pl.BlockSpec((1, D), lambda i: (0, 0)),
                  mod_spec(4),
                  mod_spec(3)],
        out_specs=(pl.BlockSpec((tm, D), lambda i: (i, 0)),
                   pl.BlockSpec((tm, D), lambda i: (i, 0))),
        compiler_params=_params(("parallel",)),
        name="out_proj",
    )(merged, w_out, x2d, mod4, g2, mod4, mod4)


def _ffn_kernel(h_ref, w1_ref, w2_ref, x1_ref, gate_ref, g_ref, o_ref, acc_ref):
    f = pl.program_id(1)

    @pl.when(f == 0)
    def _():
        acc_ref[...] = jnp.zeros_like(acc_ref)

    a = jnp.maximum(_dot(h_ref[...], w1_ref[...]), 0.0)
    acc_ref[...] += _dot((a * a).astype(BF16), w2_ref[...])

    @pl.when(f == pl.num_programs(1) - 1)
    def _():
        gate = gate_ref[...]
        g = g_ref[...]

        def body(r):
            x2 = x1_ref[r, :] + gate * acc_ref[r, :]
            o_ref[r, :] = x2 * lax.rsqrt(jnp.mean(x2 * x2, axis=-1, keepdims=True) + EPS) * g

        _for_row_slabs(x1_ref.shape[0], body)


def _ffn(h2, w1, w2, x1, mod4, g_final, S, tm=512, tf=1024):
    T, D = x1.shape
    F = w1.shape[1]
    per_b = S // tm
    return pl.pallas_call(
        _ffn_kernel,
        out_shape=jax.ShapeDtypeStruct((T, D), F32),
        grid=(T // tm, F // tf),
        in_specs=[pl.BlockSpec((tm, D), lambda i, f: (i, 0)),
                  pl.BlockSpec((D, tf), lambda i, f: (0, f)),
                  pl.BlockSpec((tf, D), lambda i, f: (f, 0)),
                  pl.BlockSpec((tm, D), lambda i, f: (i, 0)),
                  pl.BlockSpec((None, None, 1, D), lambda i, f: (i // per_b, 5, 0, 0)),
                  pl.BlockSpec((1, D), lambda i, f: (0, 0))],
        out_specs=pl.BlockSpec((tm, D), lambda i, f: (i, 0)),
        scratch_shapes=[pltpu.VMEM((tm, D), F32)],
        compiler_params=_params(("parallel", "arbitrary")),
        name="ffn",
    )(h2, w1, w2, x1, mod4, g_final)


def _pad_to(a, rows, cols):
    return jnp.pad(a, ((0, rows - a.shape[0]), (0, cols - a.shape[1])))


def _pack_w_in(w, D):
    sizes = (D // 2, D // 2, D, D, GLA_RANK, D, D, D, FOX_HEADS, D, D)
    offs = [0]
    for s in sizes:
        offs.append(offs[-1] + s)
    seg = lambda i: w[:, offs[i]:offs[i + 1]]
    main = jnp.concatenate([seg(i) for i in (0, 1, 2, 3, 5, 6, 7, 9, 10)], axis=1).astype(BF16)
    low = jnp.concatenate([_pad_to(seg(4), D, LANES), _pad_to(seg(8), D, LANES)], axis=1).astype(BF16)
    return main, low


def _layer(x2d, cond_mod, l, B, S, D, w_ada, b_ada, g_norm1, w_in, w_gla_a_up, b_gla_a, g_gla_out,
           b_fox_f, w_proj_gla, w_proj_fox, w_out, g_norm2, w_ff1, w_ff2):
    del w_ada, b_ada
    mod4 = cond_mod.reshape(B, N_MOD, 1, D)
    w_main, w_low = _pack_w_in(w_in[l], D)
    proj, low = _in_proj(x2d, g_norm1[l][None, :], mod4, w_main, w_low, S)

    wup_pad = _pad_to(w_gla_a_up[l], LANES, D // 2).astype(BF16)
    o_a = _gla(proj, low, wup_pad, b_gla_a[l][None, :], g_gla_out[l][None, :], B, S, D)

    bf_pad = _pad_to(b_fox_f[l][None, :], 1, LANES)
    cum_col, cum_row = _fox_cum(low, bf_pad, B, S)
    o_b = _fox(proj, cum_col, cum_row, B, S, D)

    merged = _merge(o_a, o_b, w_proj_gla[l].astype(BF16), w_proj_fox[l].astype(BF16), proj, D)
    x1, h2 = _out_proj(merged, w_out[l].astype(BF16), x2d, mod4, g_norm2[l][None, :], S)
    return x1, h2, mod4


def kernel(x, c, w_ada, b_ada, g_norm1, w_in, w_gla_a_up, b_gla_a, g_gla_out, b_fox_f, w_proj_gla,
           w_proj_fox, w_out, g_norm2, w_ff1, w_ff2, g_final):
    B, S, D = x.shape
    depth = w_ada.shape[0]
    assert depth == 1, "the final norm is fused into the last layer's MLP kernel"
    x2d = x.reshape(B * S, D)
    l = 0
    mod = _ada(c, w_ada[l], b_ada[l][None, :])
    x1, h2, mod4 = _layer(x2d, mod, l, B, S, D, w_ada, b_ada, g_norm1, w_in, w_gla_a_up, b_gla_a,
                          g_gla_out, b_fox_f, w_proj_gla, w_proj_fox, w_out, g_norm2, w_ff1, w_ff2)
    out = _ffn(h2, w_ff1[l].astype(BF16), w_ff2[l].astype(BF16), x1, mod4, g_final[None, :], S)
    return out.reshape(B, S, D)
```

```python
import functools
import math

import jax
import jax.numpy as jnp
from jax import lax
from jax.experimental import pallas as pl
from jax.experimental.pallas import tpu as pltpu

F32 = jnp.float32
BF16 = jnp.bfloat16

EPS = 1e-6
LANES = 128
GLA_HEADS = 4
GLA_RANK = 16
GLA_TAU = 16.0
GLA_CHUNK = 64
FOX_HEADS = 16
N_MOD = 6
NEG_BIG = -1e30
LOG2E = math.log2(math.e)
SLAB = 16
SLAB_UNROLL = 4
VMEM_LIMIT = 56 * 1024 * 1024

_NT = (((1,), (1,)), ((), ()))
_TN = (((0,), (0,)), ((), ()))


def _dot(a, b):
    return jnp.dot(a, b, preferred_element_type=F32)


def _log_sigmoid(z):
    return jnp.minimum(z, 0.0) - jnp.log(1.0 + jnp.exp(-jnp.abs(z)))


def _silu(z):
    return z * jax.nn.sigmoid(z)


def _split_bf16(a):
    hi = a.astype(BF16)
    lo = (a - hi.astype(F32)).astype(BF16)
    return hi, lo


def _params(sem):
    return pltpu.CompilerParams(dimension_semantics=sem, vmem_limit_bytes=VMEM_LIMIT)


def _ada_kernel(c_ref, w_ref, b_ref, o_ref):
    cond = _silu(c_ref[...]).astype(BF16)
    o_ref[...] = _dot(cond, w_ref[...].astype(BF16)) + b_ref[...]


def _ada(c, w_ada, b_ada, tn=1024):
    B, D = c.shape
    N = w_ada.shape[1]
    return pl.pallas_call(
        _ada_kernel,
        out_shape=jax.ShapeDtypeStruct((B, N), F32),
        grid=(N // tn,),
        in_specs=[pl.BlockSpec((B, D), lambda j: (0, 0)),
                  pl.BlockSpec((D, tn), lambda j: (0, j)),
                  pl.BlockSpec((1, tn), lambda j: (0, j))],
        out_specs=pl.BlockSpec((B, tn), lambda j: (0, j)),
        compiler_params=_params(("arbitrary",)),
        name="ada",
    )(c, w_ada, b_ada)


def _modulated_norm_rows(x, gain, shift):
    return x * lax.rsqrt(jnp.mean(x * x, axis=-1, keepdims=True) + EPS) * gain + shift


def _for_row_slabs(n_rows, body):
    def step(s, carry):
        body(pl.ds(pl.multiple_of(s * SLAB, SLAB), SLAB))
        return carry

    lax.fori_loop(0, n_rows // SLAB, step, 0, unroll=SLAB_UNROLL)


def _in_kernel(x_ref, g_ref, scale_ref, shift_ref, wt_ref, wa_ref, wb_ref, proj_ref, low_ref, h_ref):
    @pl.when(pl.program_id(1) == 0)
    def _():
        gain = g_ref[...] * (1.0 + scale_ref[...])
        shift = shift_ref[...]

        def body(r):
            h_ref[r, :] = _modulated_norm_rows(x_ref[r, :], gain, shift).astype(BF16)

        _for_row_slabs(x_ref.shape[0], body)
        lane = lax.broadcasted_iota(jnp.int32, (x_ref.shape[0], LANES), 1)
        for n, (w_ref, width) in enumerate(((wa_ref, GLA_RANK), (wb_ref, FOX_HEADS))):
            seg = lax.dot_general(h_ref[...], w_ref[...].astype(BF16), _NT, preferred_element_type=F32)
            low_ref[:, n * LANES:(n + 1) * LANES] = jnp.where(lane < width, seg, 0.0)

    w_t = wt_ref[...].astype(BF16)
    proj_ref[...] = lax.dot_general(h_ref[...], w_t, _NT, preferred_element_type=F32).astype(proj_ref.dtype)


def _in_proj(x2d, g1, mod4, w_t, S, tm=1024, tn=1024):
    T, D = x2d.shape
    N = 8 * D
    per_b = S // tm
    align = math.gcd(tn, GLA_RANK, FOX_HEADS)

    def w_rows(i, j):
        skip = jnp.where(j >= 3 * D // tn, GLA_RANK, 0) + jnp.where(j >= 6 * D // tn, FOX_HEADS, 0)
        return (pl.multiple_of(j * tn + skip, align), 0)

    narrow_spec = lambda row0: pl.BlockSpec((pl.Element(LANES), pl.Element(D)), lambda i, j: (row0, 0))
    return pl.pallas_call(
        _in_kernel,
        out_shape=(jax.ShapeDtypeStruct((T, N), BF16), jax.ShapeDtypeStruct((T, 2 * LANES), F32)),
        grid=(T // tm, N // tn),
        in_specs=[pl.BlockSpec((tm, D), lambda i, j: (i, 0)),
                  pl.BlockSpec((1, D), lambda i, j: (0, 0)),
                  pl.BlockSpec((None, None, 1, D), lambda i, j: (i // per_b, 1, 0, 0)),
                  pl.BlockSpec((None, None, 1, D), lambda i, j: (i // per_b, 0, 0, 0)),
                  pl.BlockSpec((pl.Element(tn), pl.Element(D)), w_rows),
                  narrow_spec(3 * D),
                  narrow_spec(6 * D + GLA_RANK)],
        out_specs=(pl.BlockSpec((tm, tn), lambda i, j: (i, j)),
                   pl.BlockSpec((tm, 2 * LANES), lambda i, j: (i, 0))),
        scratch_shapes=[pltpu.VMEM((tm, D), BF16)],
        compiler_params=_params(("parallel", "arbitrary")),
        name="in_proj",
    )(x2d, g1, mod4, mod4, w_t, w_t, w_t)


def _gla_kernel(q_ref, k_ref, v_ref, gg_ref, low_ref, wup_ref, ba_ref, g_ref, o_ref, *, q_scale, chunk):
    C, SB = chunk, GLA_CHUNK
    S, dk = q_ref.shape
    dv = v_ref.shape[1]
    tril = (lax.broadcasted_iota(jnp.int32, (C, C), 1) <= lax.broadcasted_iota(jnp.int32, (C, C), 0)).astype(BF16)
    wup = wup_ref[...]
    ba = ba_ref[...]
    g = g_ref[...]
    state_t = jnp.zeros((dv, dk), F32)
    for c in range(S // C):
        rows = slice(c * C, (c + 1) * C)
        q = q_ref[rows, :].astype(F32) * q_scale
        k = k_ref[rows, :].astype(F32)
        v = v_ref[rows, :]
        z = _dot(low_ref[rows, :].astype(BF16), wup) + ba
        la = _log_sigmoid(z) * (1.0 / GLA_TAU)
        la_hi, la_lo = _split_bf16(la)
        b = _dot(tril, la_hi) + _dot(tril, la_lo)
        b_last = b[C - 1:C, :]
        q_dec = (q * jnp.exp(b)).astype(BF16)
        k_dec = (k * jnp.exp(b_last - b)).astype(BF16)

        attn_rows = []
        for i in range(C // SB):
            sub = slice(i * SB, (i + 1) * SB)
            n_keys = -(-(i + 1) * SB // LANES) * LANES
            ref = b[i * SB - 1:i * SB, :] if i > 0 else jnp.zeros((1, dk), F32)
            q_loc = (q[sub] * jnp.exp(b[sub] - ref)).astype(BF16)
            k_loc = (k[:n_keys] * jnp.exp(ref - b[:n_keys])).astype(BF16)
            a = lax.dot_general(q_loc, k_loc, _NT, preferred_element_type=F32)
            key_i = lax.broadcasted_iota(jnp.int32, (SB, n_keys), 1)
            qry_i = lax.broadcasted_iota(jnp.int32, (SB, n_keys), 0) + i * SB
            a = jnp.where(key_i <= qry_i, a, 0.0)
            if n_keys < C:
                a = jnp.concatenate([a, jnp.zeros((SB, C - n_keys), F32)], axis=1)
            attn_rows.append(a)
        attn = jnp.concatenate(attn_rows, axis=0).astype(BF16)

        o = _dot(attn, v) + lax.dot_general(q_dec, state_t.astype(BF16), _NT, preferred_element_type=F32)
        state_t = state_t * jnp.exp(b_last) + lax.dot_general(v, k_dec, _TN, preferred_element_type=F32)
        y = o * lax.rsqrt(jnp.mean(o * o, axis=-1, keepdims=True) + EPS) * g
        o_ref[rows, :] = (y * _silu(gg_ref[rows, :].astype(F32))).astype(o_ref.dtype)


def _gla(proj, low, wup_pad, b_a, g_out, B, S, D, chunk=256):
    T = B * S
    H = GLA_HEADS
    dk = (D // 2) // H
    dv = D // H
    k_blk0 = (D // 2) // dk
    v_blk0 = D // dv
    gg_blk0 = 2 * D // dv
    return pl.pallas_call(
        functools.partial(_gla_kernel, q_scale=dk ** -0.5, chunk=chunk),
        out_shape=jax.ShapeDtypeStruct((T, D), BF16),
        grid=(B, H),
        in_specs=[pl.BlockSpec((S, dk), lambda b, h: (b, h)),
                  pl.BlockSpec((S, dk), lambda b, h: (b, k_blk0 + h)),
                  pl.BlockSpec((S, dv), lambda b, h: (b, v_blk0 + h)),
                  pl.BlockSpec((S, dv), lambda b, h: (b, gg_blk0 + h)),
                  pl.BlockSpec((S, LANES), lambda b, h: (b, 0)),
                  pl.BlockSpec((LANES, dk), lambda b, h: (0, h)),
                  pl.BlockSpec((1, dk), lambda b, h: (0, h)),
                  pl.BlockSpec((1, dv), lambda b, h: (0, 0))],
        out_specs=pl.BlockSpec((S, dv), lambda b, h: (b, h)),
        compiler_params=_params(("parallel", "arbitrary")),
        name="gla",
    )(proj, proj, proj, proj, low, wup_pad, b_a, g_out)


def _cum_kernel(low_ref, bf_ref, col_ref, row_ref, *, blk):
    S = low_ref.shape[0]
    n_heads = row_ref.shape[0]
    r_i = lax.broadcasted_iota(jnp.int32, (blk, blk), 0)
    c_i = lax.broadcasted_iota(jnp.int32, (blk, blk), 1)
    tril = (c_i <= r_i).astype(BF16)
    triu = (r_i <= c_i).astype(BF16)
    bf = bf_ref[...]
    carry_row = jnp.zeros((1, LANES), F32)
    carry_col = jnp.zeros((LANES, 1), F32)
    for i in range(S // blk):
        r = slice(i * blk, (i + 1) * blk)
        lf = _log_sigmoid(low_ref[r, :] + bf)
        hi, lo = _split_bf16(lf)
        loc = _dot(tril, hi) + _dot(tril, lo) + carry_row
        col_ref[r, :] = loc
        carry_row = loc[blk - 1:blk, :]
        loc_t = (lax.dot_general(hi, triu, _TN, preferred_element_type=F32)
                 + lax.dot_general(lo, triu, _TN, preferred_element_type=F32) + carry_col)
        row_ref[:, r] = loc_t[:n_heads, :]
        carry_col = loc_t[:, blk - 1:blk]


def _fox_cum(low, bf_pad, B, S, blk=128):
    return pl.pallas_call(
        functools.partial(_cum_kernel, blk=blk),
        out_shape=(jax.ShapeDtypeStruct((B, S, LANES), F32),
                   jax.ShapeDtypeStruct((B, FOX_HEADS, S), F32)),
        grid=(B,),
        in_specs=[pl.BlockSpec((S, LANES), lambda b: (b, 1)),
                  pl.BlockSpec((1, LANES), lambda b: (0, 0))],
        out_specs=(pl.BlockSpec((None, S, LANES), lambda b: (b, 0, 0)),
                   pl.BlockSpec((None, FOX_HEADS, S), lambda b: (b, 0, 0))),
        compiler_params=_params(("arbitrary",)),
        name="fox_cum",
    )(low, bf_pad)


def _fox_kernel(q_ref, k_ref, v_ref, cc_ref, cr_ref, o_ref, qs_ref, m_ref, l_ref, acc_ref, *, blk, q_tile, scale):
    S, hd = q_ref.shape
    h = pl.program_id(1)
    qs_ref[...] = (q_ref[...].astype(F32) * (scale * LOG2E)).astype(BF16)
    lane = lax.broadcasted_iota(jnp.int32, (S, LANES), 1)
    ck_all = jnp.sum(jnp.where(lane == h, cc_ref[...], 0.0), axis=-1, keepdims=True) * LOG2E
    cq_all = cr_ref[pl.ds(h, 1), :] * LOG2E
    key_i = lax.broadcasted_iota(jnp.int32, (blk, blk), 0)
    qry_i = lax.broadcasted_iota(jnp.int32, (blk, blk), 1)
    causal = key_i <= qry_i

    n_blk = S // blk
    for j in range(n_blk):
        ks = slice(j * blk, (j + 1) * blk)
        k_j = k_ref[ks, :]
        vt_j = v_ref[ks, :].T
        ck = ck_all[ks]
        for q0 in range(0, S, q_tile):
            a = max(q0, j * blk)
            if a >= q0 + q_tile:
                continue
            qs = slice(a, q0 + q_tile)
            t = lax.dot_general(k_j, qs_ref[qs, :], _NT, preferred_element_type=F32) - ck
            if a == j * blk:
                t_diag = jnp.where(causal, t[:, :blk], NEG_BIG)
                t = t_diag if a + blk == q0 + q_tile else jnp.concatenate([t_diag, t[:, blk:]], axis=1)
            cq = cq_all[:, qs]
            t_max = jnp.max(t, axis=0, keepdims=True) + cq
            if j == 0:
                m_new = t_max
            else:
                m_old = m_ref[:, qs]
                m_new = jnp.maximum(m_old, t_max)
                alpha = jnp.exp2(m_old - m_new)
            p = jnp.exp2(t - (m_new - cq))
            p_sum = jnp.sum(p, axis=0, keepdims=True)
            pv = _dot(vt_j, p.astype(BF16))
            if j == 0:
                l_ref[:, qs] = p_sum
                acc_ref[:, qs] = pv
            else:
                l_ref[:, qs] = alpha * l_ref[:, qs] + p_sum
                acc_ref[:, qs] = alpha * acc_ref[:, qs] + pv
            m_ref[:, qs] = m_new
    o_ref[...] = (acc_ref[...] / l_ref[...]).T.astype(o_ref.dtype)


def _fox(proj, cum_col, cum_row, B, S, D, blk=256, q_tile=2048):
    T = B * S
    H = FOX_HEADS
    hd = D // H
    q_blk0, k_blk0, v_blk0 = 3 * D // hd, 4 * D // hd, 5 * D // hd
    return pl.pallas_call(
        functools.partial(_fox_kernel, blk=blk, q_tile=q_tile, scale=1.0 / math.sqrt(hd)),
        out_shape=jax.ShapeDtypeStruct((T, D), BF16),
        grid=(B, H),
        in_specs=[pl.BlockSpec((S, hd), lambda b, h: (b, q_blk0 + h)),
                  pl.BlockSpec((S, hd), lambda b, h: (b, k_blk0 + h)),
                  pl.BlockSpec((S, hd), lambda b, h: (b, v_blk0 + h)),
                  pl.BlockSpec((None, S, LANES), lambda b, h: (b, 0, 0)),
                  pl.BlockSpec((None, H, S), lambda b, h: (b, 0, 0))],
        out_specs=pl.BlockSpec((S, hd), lambda b, h: (b, h)),
        scratch_shapes=[pltpu.VMEM((S, hd), BF16), pltpu.VMEM((1, S), F32), pltpu.VMEM((1, S), F32),
                        pltpu.VMEM((hd, S), F32)],
        compiler_params=_params(("parallel", "arbitrary")),
        name="fox",
    )(proj, proj, proj, cum_col, cum_row)


def _merge_kernel(oa_ref, ob_ref, wa_ref, wb_ref, ga_ref, gb_ref, o_ref):
    a = _dot(oa_ref[...], wa_ref[...])
    b = _dot(ob_ref[...], wb_ref[...])
    merged = jax.nn.sigmoid(ga_ref[...].astype(F32)) * a + jax.nn.sigmoid(gb_ref[...].astype(F32)) * b
    o_ref[...] = merged.astype(o_ref.dtype)


def _merge(o_a, o_b, w_a, w_b, proj, D, tm=512, tn=1024):
    T = o_a.shape[0]
    ga_blk0, gb_blk0 = 6 * D // tn, 7 * D // tn
    return pl.pallas_call(
        _merge_kernel,
        out_shape=jax.ShapeDtypeStruct((T, D), BF16),
        grid=(D // tn, T // tm),
        in_specs=[pl.BlockSpec((tm, D), lambda j, i: (i, 0)),
                  pl.BlockSpec((tm, D), lambda j, i: (i, 0)),
                  pl.BlockSpec((D, tn), lambda j, i: (0, j)),
                  pl.BlockSpec((D, tn), lambda j, i: (0, j)),
                  pl.BlockSpec((tm, tn), lambda j, i: (i, ga_blk0 + j)),
                  pl.BlockSpec((tm, tn), lambda j, i: (i, gb_blk0 + j))],
        out_specs=pl.BlockSpec((tm, tn), lambda j, i: (i, j)),
        compiler_params=_params(("arbitrary", "parallel")),
        name="merge",
    )(o_a, o_b, w_a, w_b, proj, proj)


def _out_kernel(m_ref, w_ref, x_ref, gate_ref, g_ref, scale_ref, shift_ref, x1_ref, h2_ref):
    x1_ref[...] = _dot(m_ref[...], w_ref[...])
    gate = gate_ref[...]
    gain = g_ref[...] * (1.0 + scale_ref[...])
    shift = shift_ref[...]

    def body(r):
        x1 = x_ref[r, :] + gate * x1_ref[r, :]
        x1_ref[r, :] = x1
        h2_ref[r, :] = _modulated_norm_rows(x1, gain, shift).astype(BF16)

    _for_row_slabs(x_ref.shape[0], body)


def _out_proj(merged, w_out, x2d, mod4, g2, S, tm=512):
    T, D = x2d.shape
    per_b = S // tm
    mod_spec = lambda k: pl.BlockSpec((None, None, 1, D), lambda i: (i // per_b, k, 0, 0))
    return pl.pallas_call(
        _out_kernel,
        out_shape=(jax.ShapeDtypeStruct((T, D), F32), jax.ShapeDtypeStruct((T, D), BF16)),
        grid=(T // tm,),
        in_specs=[pl.BlockSpec((tm, D), lambda i: (i, 0)),
                  pl.BlockSpec((D, D), lambda i: (0, 0)),
                  pl.BlockSpec((tm, D), lambda i: (i, 0)),
                  mod_spec(2),
                  pl.BlockSpec((1, D), lambda i: (0, 0)),
                  mod_spec(4),
                  mod_spec(3)],
        out_specs=(pl.BlockSpec((tm, D), lambda i: (i, 0)),
                   pl.BlockSpec((tm, D), lambda i: (i, 0))),
        compiler_params=_params(("parallel",)),
        name="out_proj",
    )(merged, w_out, x2d, mod4, g2, mod4, mod4)


def _ffn_kernel(h_ref, w1_ref, w2_ref, x1_ref, gate_ref, g_ref, o_ref, acc_ref):
    f = pl.program_id(1)

    @pl.when(f == 0)
    def _():
        acc_ref[...] = jnp.zeros_like(acc_ref)

    a = jnp.maximum(_dot(h_ref[...], w1_ref[...]), 0.0)
    acc_ref[...] += _dot((a * a).astype(BF16), w2_ref[...])

    @pl.when(f == pl.num_programs(1) - 1)
    def _():
        gate = gate_ref[...]
        g = g_ref[...]

        def body(r):
            x2 = x1_ref[r, :] + gate * acc_ref[r, :]
            o_ref[r, :] = x2 * lax.rsqrt(jnp.mean(x2 * x2, axis=-1, keepdims=True) + EPS) * g

        _for_row_slabs(x1_ref.shape[0], body)


def _ffn(h2, w1, w2, x1, mod4, g_final, S, tm=512, tf=1024):
    T, D = x1.shape
    F = w1.shape[1]
    per_b = S // tm
    return pl.pallas_call(
        _ffn_kernel,
        out_shape=jax.ShapeDtypeStruct((T, D), F32),
        grid=(T // tm, F // tf),
        in_specs=[pl.BlockSpec((tm, D), lambda i, f: (i, 0)),
                  pl.BlockSpec((D, tf), lambda i, f: (0, f)),
                  pl.BlockSpec((tf, D), lambda i, f: (f, 0)),
                  pl.BlockSpec((tm, D), lambda i, f: (i, 0)),
                  pl.BlockSpec((None, None, 1, D), lambda i, f: (i // per_b, 5, 0, 0)),
                  pl.BlockSpec((1, D), lambda i, f: (0, 0))],
        out_specs=pl.BlockSpec((tm, D), lambda i, f: (i, 0)),
        scratch_shapes=[pltpu.VMEM((tm, D), F32)],
        compiler_params=_params(("parallel", "arbitrary")),
        name="ffn",
    )(h2, w1, w2, x1, mod4, g_final)


def _pad_to(a, rows, cols):
    return jnp.pad(a, ((0, rows - a.shape[0]), (0, cols - a.shape[1])))


def kernel(x, c, w_ada, b_ada, g_norm1, w_in, w_gla_a_up, b_gla_a, g_gla_out, b_fox_f, w_proj_gla,
           w_proj_fox, w_out, g_norm2, w_ff1, w_ff2, g_final):
    B, S, D = x.shape
    assert w_ada.shape[0] == 1, "the final norm is fused into the last layer's MLP kernel"
    l = 0
    x2d = x.reshape(B * S, D)
    mod4 = _ada(c, w_ada[l], b_ada[l][None, :]).reshape(B, N_MOD, 1, D)

    proj, low = _in_proj(x2d, g_norm1[l][None, :], mod4, jnp.swapaxes(w_in[l], 0, 1), S)

    wup_pad = _pad_to(w_gla_a_up[l], LANES, D // 2).astype(BF16)
    o_a = _gla(proj, low, wup_pad, b_gla_a[l][None, :], g_gla_out[l][None, :], B, S, D)

    bf_pad = _pad_to(b_fox_f[l][None, :], 1, LANES)
    cum_col, cum_row = _fox_cum(low, bf_pad, B, S)
    o_b = _fox(proj, cum_col, cum_row, B, S, D)

    merged = _merge(o_a, o_b, w_proj_gla[l].astype(BF16), w_proj_fox[l].astype(BF16), proj, D)
    x1, h2 = _out_proj(merged, w_out[l].astype(BF16), x2d, mod4, g_norm2[l][None, :], S)
    out = _ffn(h2, w_ff1[l].astype(BF16), w_ff2[l].astype(BF16), x1, mod4, g_final[None, :], S)
    return out.reshape(B, S, D)
```

```python
import functools
import math

import jax
import jax.numpy as jnp
from jax import lax
from jax.experimental import pallas as pl
from jax.experimental.pallas import tpu as pltpu

F32 = jnp.float32
BF16 = jnp.bfloat16

EPS = 1e-6
LANES = 128
GLA_HEADS = 4
GLA_RANK = 16
GLA_TAU = 16.0
GLA_CHUNK = 64
FOX_HEADS = 16
N_MOD = 6
NEG_BIG = -1e30
LOG2E = math.log2(math.e)
SLAB = 16
SLAB_UNROLL = 4
VMEM_LIMIT = 56 * 1024 * 1024

_NT = (((1,), (1,)), ((), ()))
_TN = (((0,), (0,)), ((), ()))


def _dot(a, b):
    return jnp.dot(a, b, preferred_element_type=F32)


def _log_sigmoid(z):
    return jnp.minimum(z, 0.0) - jnp.log(1.0 + jnp.exp(-jnp.abs(z)))


def _silu(z):
    return z * jax.nn.sigmoid(z)


def _split_bf16(a):
    hi = a.astype(BF16)
    lo = (a - hi.astype(F32)).astype(BF16)
    return hi, lo


def _params(sem):
    return pltpu.CompilerParams(dimension_semantics=sem, vmem_limit_bytes=VMEM_LIMIT)


def _ada_kernel(c_ref, w_ref, b_ref, o_ref):
    cond = _silu(c_ref[...]).astype(BF16)
    o_ref[...] = _dot(cond, w_ref[...].astype(BF16)) + b_ref[...]


def _ada(c, w_ada, b_ada, tn=1024):
    B, D = c.shape
    N = w_ada.shape[1]
    return pl.pallas_call(
        _ada_kernel,
        out_shape=jax.ShapeDtypeStruct((B, N), F32),
        grid=(N // tn,),
        in_specs=[pl.BlockSpec((B, D), lambda j: (0, 0)),
                  pl.BlockSpec((D, tn), lambda j: (0, j)),
                  pl.BlockSpec((1, tn), lambda j: (0, j))],
        out_specs=pl.BlockSpec((B, tn), lambda j: (0, j)),
        compiler_params=_params(("arbitrary",)),
        name="ada",
    )(c, w_ada, b_ada)


def _modulated_norm_rows(x, gain, shift):
    return x * lax.rsqrt(jnp.mean(x * x, axis=-1, keepdims=True) + EPS) * gain + shift


def _for_row_slabs(n_rows, body):
    def step(s, carry):
        body(pl.ds(pl.multiple_of(s * SLAB, SLAB), SLAB))
        return carry

    lax.fori_loop(0, n_rows // SLAB, step, 0, unroll=SLAB_UNROLL)


def _in_kernel(x_ref, g_ref, scale_ref, shift_ref, wt_ref, wa_ref, wb_ref, proj_ref, low_ref, h_ref):
    @pl.when(pl.program_id(1) == 0)
    def _():
        gain = g_ref[...] * (1.0 + scale_ref[...])
        shift = shift_ref[...]

        def body(r):
            h_ref[r, :] = _modulated_norm_rows(x_ref[r, :], gain, shift).astype(BF16)

        _for_row_slabs(x_ref.shape[0], body)
        lane = lax.broadcasted_iota(jnp.int32, (x_ref.shape[0], LANES), 1)
        for n, (w_ref, width) in enumerate(((wa_ref, GLA_RANK), (wb_ref, FOX_HEADS))):
            seg = lax.dot_general(h_ref[...], w_ref[...].astype(BF16), _NT, preferred_element_type=F32)
            low_ref[:, n * LANES:(n + 1) * LANES] = jnp.where(lane < width, seg, 0.0)

    w_t = wt_ref[...].astype(BF16)
    proj_ref[...] = lax.dot_general(h_ref[...], w_t, _NT, preferred_element_type=F32).astype(proj_ref.dtype)


def _in_proj(x2d, g1, mod4, w_t, S, tm=1024, tn=1024):
    T, D = x2d.shape
    N = 8 * D
    per_b = S // tm
    align = math.gcd(tn, GLA_RANK, FOX_HEADS)

    def w_rows(i, j):
        skip = jnp.where(j >= 3 * D // tn, GLA_RANK, 0) + jnp.where(j >= 6 * D // tn, FOX_HEADS, 0)
        return (pl.multiple_of(j * tn + skip, align), 0)

    narrow_spec = lambda row0: pl.BlockSpec((pl.Element(LANES), pl.Element(D)), lambda i, j: (row0, 0))
    return pl.pallas_call(
        _in_kernel,
        out_shape=(jax.ShapeDtypeStruct((T, N), BF16), jax.ShapeDtypeStruct((T, 2 * LANES), F32)),
        grid=(T // tm, N // tn),
        in_specs=[pl.BlockSpec((tm, D), lambda i, j: (i, 0)),
                  pl.BlockSpec((1, D), lambda i, j: (0, 0)),
                  pl.BlockSpec((None, None, 1, D), lambda i, j: (i // per_b, 1, 0, 0)),
                  pl.BlockSpec((None, None, 1, D), lambda i, j: (i // per_b, 0, 0, 0)),
                  pl.BlockSpec((pl.Element(tn), pl.Element(D)), w_rows),
                  narrow_spec(3 * D),
                  narrow_spec(6 * D + GLA_RANK)],
        out_specs=(pl.BlockSpec((tm, tn), lambda i, j: (i, j)),
                   pl.BlockSpec((tm, 2 * LANES), lambda i, j: (i, 0))),
        scratch_shapes=[pltpu.VMEM((tm, D), BF16)],
        compiler_params=_params(("parallel", "arbitrary")),
        name="in_proj",
    )(x2d, g1, mod4, mod4, w_t, w_t, w_t)


def _gla_kernel(q_ref, k_ref, v_ref, gg_ref, low_ref, wup_ref, ba_ref, g_ref, o_ref, *, q_scale, chunk):
    C, SB = chunk, GLA_CHUNK
    S, dk = q_ref.shape
    dv = v_ref.shape[1]
    tril = (lax.broadcasted_iota(jnp.int32, (C, C), 1) <= lax.broadcasted_iota(jnp.int32, (C, C), 0)).astype(BF16)
    wup = wup_ref[...]
    ba = ba_ref[...]
    g = g_ref[...]
    state_t = jnp.zeros((dv, dk), F32)
    for c in range(S // C):
        rows = slice(c * C, (c + 1) * C)
        q = q_ref[rows, :].astype(F32) * q_scale
        k = k_ref[rows, :].astype(F32)
        v = v_ref[rows, :]
        z = _dot(low_ref[rows, :].astype(BF16), wup) + ba
        la = _log_sigmoid(z) * (1.0 / GLA_TAU)
        la_hi, la_lo = _split_bf16(la)
        b = _dot(tril, la_hi) + _dot(tril, la_lo)
        b_last = b[C - 1:C, :]
        q_dec = (q * jnp.exp(b)).astype(BF16)
        k_dec = (k * jnp.exp(b_last - b)).astype(BF16)

        attn_rows = []
        for i in range(C // SB):
            sub = slice(i * SB, (i + 1) * SB)
            n_keys = -(-(i + 1) * SB // LANES) * LANES
            ref = b[i * SB - 1:i * SB, :] if i > 0 else jnp.zeros((1, dk), F32)
            q_loc = (q[sub] * jnp.exp(b[sub] - ref)).astype(BF16)
            k_loc = (k[:n_keys] * jnp.exp(ref - b[:n_keys])).astype(BF16)
            a = lax.dot_general(q_loc, k_loc, _NT, preferred_element_type=F32)
            key_i = lax.broadcasted_iota(jnp.int32, (SB, n_keys), 1)
            qry_i = lax.broadcasted_iota(jnp.int32, (SB, n_keys), 0) + i * SB
            a = jnp.where(key_i <= qry_i, a, 0.0)
            if n_keys < C:
                a = jnp.concatenate([a, jnp.zeros((SB, C - n_keys), F32)], axis=1)
            attn_rows.append(a)
        attn = jnp.concatenate(attn_rows, axis=0).astype(BF16)

        o = _dot(attn, v) + lax.dot_general(q_dec, state_t.astype(BF16), _NT, preferred_element_type=F32)
        state_t = state_t * jnp.exp(b_last) + lax.dot_general(v, k_dec, _TN, preferred_element_type=F32)
        y = o * lax.rsqrt(jnp.mean(o * o, axis=-1, keepdims=True) + EPS) * g
        o_ref[rows, :] = (y * _silu(gg_ref[rows, :].astype(F32))).astype(o_ref.dtype)


def _gla(proj, low, wup_pad, b_a, g_out, B, S, D, chunk=256):
    T = B * S
    H = GLA_HEADS
    dk = (D // 2) // H
    dv = D // H
    k_blk0 = (D // 2) // dk
    v_blk0 = D // dv
    gg_blk0 = 2 * D // dv
    return pl.pallas_call(
        functools.partial(_gla_kernel, q_scale=dk ** -0.5, chunk=chunk),
        out_shape=jax.ShapeDtypeStruct((T, D), BF16),
        grid=(B, H),
        in_specs=[pl.BlockSpec((S, dk), lambda b, h: (b, h)),
                  pl.BlockSpec((S, dk), lambda b, h: (b, k_blk0 + h)),
                  pl.BlockSpec((S, dv), lambda b, h: (b, v_blk0 + h)),
                  pl.BlockSpec((S, dv), lambda b, h: (b, gg_blk0 + h)),
                  pl.BlockSpec((S, LANES), lambda b, h: (b, 0)),
                  pl.BlockSpec((LANES, dk), lambda b, h: (0, h)),
                  pl.BlockSpec((1, dk), lambda b, h: (0, h)),
                  pl.BlockSpec((1, dv), lambda b, h: (0, 0))],
        out_specs=pl.BlockSpec((S, dv), lambda b, h: (b, h)),
        compiler_params=_params(("parallel", "arbitrary")),
        name="gla",
    )(proj, proj, proj, proj, low, wup_pad, b_a, g_out)


def _cum_kernel(low_ref, bf_ref, col_ref, row_ref, *, blk):
    S = low_ref.shape[0]
    n_heads = row_ref.shape[0]
    r_i = lax.broadcasted_iota(jnp.int32, (blk, blk), 0)
    c_i = lax.broadcasted_iota(jnp.int32, (blk, blk), 1)
    tril = (c_i <= r_i).astype(BF16)
    triu = (r_i <= c_i).astype(BF16)
    bf = bf_ref[...]
    carry_row = jnp.zeros((1, LANES), F32)
    carry_col = jnp.zeros((LANES, 1), F32)
    for i in range(S // blk):
        r = slice(i * blk, (i + 1) * blk)
        lf = _log_sigmoid(low_ref[r, :] + bf)
        hi, lo = _split_bf16(lf)
        loc = _dot(tril, hi) + _dot(tril, lo) + carry_row
        col_ref[r, :] = loc
        carry_row = loc[blk - 1:blk, :]
        loc_t = (lax.dot_general(hi, triu, _TN, preferred_element_type=F32)
                 + lax.dot_general(lo, triu, _TN, preferred_element_type=F32) + carry_col)
        row_ref[:, r] = loc_t[:n_heads, :]
        carry_col = loc_t[:, blk - 1:blk]


def _fox_cum(low, bf_pad, B, S, blk=128):
    return pl.pallas_call(
        functools.partial(_cum_kernel, blk=blk),
        out_shape=(jax.ShapeDtypeStruct((B, S, LANES), F32),
                   jax.ShapeDtypeStruct((B, FOX_HEADS, S), F32)),
        grid=(B,),
        in_specs=[pl.BlockSpec((S, LANES), lambda b: (b, 1)),
                  pl.BlockSpec((1, LANES), lambda b: (0, 0))],
        out_specs=(pl.BlockSpec((None, S, LANES), lambda b: (b, 0, 0)),
                   pl.BlockSpec((None, FOX_HEADS, S), lambda b: (b, 0, 0))),
        compiler_params=_params(("arbitrary",)),
        name="fox_cum",
    )(low, bf_pad)


def _fox_kernel(q_ref, k_ref, v_ref, cc_ref, cr_ref, o_ref, qs_ref, m_ref, l_ref, acc_ref, *, blk, hd, scale):
    S = q_ref.shape[0]
    n_heads = q_ref.shape[1] // hd
    lane = lax.broadcasted_iota(jnp.int32, (S, LANES), 1)
    key_i = lax.broadcasted_iota(jnp.int32, (blk, blk), 0)
    qry_i = lax.broadcasted_iota(jnp.int32, (blk, blk), 1)
    causal = key_i <= qry_i

    for g in range(n_heads):
        h = pl.program_id(1) * n_heads + g
        cols = slice(g * hd, (g + 1) * hd)
        qs_ref[g] = (q_ref[:, cols].astype(F32) * (scale * LOG2E)).astype(BF16)
        ck_all = jnp.sum(jnp.where(lane == h, cc_ref[...], 0.0), axis=-1, keepdims=True) * LOG2E
        cq_all = cr_ref[pl.ds(h, 1), :] * LOG2E
        for j in range(S // blk):
            r0 = j * blk
            ks = slice(r0, r0 + blk)
            t = lax.dot_general(k_ref[ks, cols], qs_ref[g, r0:, :], _NT, preferred_element_type=F32) - ck_all[ks]
            t_diag = jnp.where(causal, t[:, :blk], NEG_BIG)
            t = t_diag if r0 + blk == S else jnp.concatenate([t_diag, t[:, blk:]], axis=1)
            cq = cq_all[:, r0:]
            t_max = jnp.max(t, axis=0, keepdims=True) + cq
            if j == 0:
                m_new = t_max
            else:
                m_old = m_ref[g, :, r0:]
                m_new = jnp.maximum(m_old, t_max)
                alpha = jnp.exp2(m_old - m_new)
            p = jnp.exp2(t - (m_new - cq))
            p_sum = jnp.sum(p, axis=0, keepdims=True)
            pv = _dot(v_ref[ks, cols].T, p.astype(BF16))
            if j == 0:
                l_ref[g] = p_sum
                acc_ref[g] = pv
            else:
                l_ref[g, :, r0:] = alpha * l_ref[g, :, r0:] + p_sum
                acc_ref[g, :, r0:] = alpha * acc_ref[g, :, r0:] + pv
            m_ref[g, :, r0:] = m_new
        o_ref[:, cols] = (acc_ref[g] / l_ref[g]).T.astype(o_ref.dtype)


def _fox(proj, cum_col, cum_row, B, S, D, blk=256, heads_per_step=4):
    T = B * S
    H = FOX_HEADS
    hd = D // H
    G = heads_per_step
    w = G * hd
    q_blk0, k_blk0, v_blk0 = 3 * D // w, 4 * D // w, 5 * D // w
    return pl.pallas_call(
        functools.partial(_fox_kernel, blk=blk, hd=hd, scale=1.0 / math.sqrt(hd)),
        out_shape=jax.ShapeDtypeStruct((T, D), BF16),
        grid=(B, H // G),
        in_specs=[pl.BlockSpec((S, w), lambda b, h: (b, q_blk0 + h)),
                  pl.BlockSpec((S, w), lambda b, h: (b, k_blk0 + h)),
                  pl.BlockSpec((S, w), lambda b, h: (b, v_blk0 + h)),
                  pl.BlockSpec((None, S, LANES), lambda b, h: (b, 0, 0)),
                  pl.BlockSpec((None, H, S), lambda b, h: (b, 0, 0))],
        out_specs=pl.BlockSpec((S, w), lambda b, h: (b, h)),
        scratch_shapes=[pltpu.VMEM((G, S, hd), BF16), pltpu.VMEM((G, 1, S), F32), pltpu.VMEM((G, 1, S), F32),
                        pltpu.VMEM((G, hd, S), F32)],
        compiler_params=_params(("parallel", "arbitrary")),
        name="fox",
    )(proj, proj, proj, cum_col, cum_row)


def _merge_kernel(oa_ref, ob_ref, wa_ref, wb_ref, ga_ref, gb_ref, o_ref):
    a = _dot(oa_ref[...], wa_ref[...])
    b = _dot(ob_ref[...], wb_ref[...])
    merged = jax.nn.sigmoid(ga_ref[...].astype(F32)) * a + jax.nn.sigmoid(gb_ref[...].astype(F32)) * b
    o_ref[...] = merged.astype(o_ref.dtype)


def _merge(o_a, o_b, w_a, w_b, proj, D, tm=512, tn=1024):
    T = o_a.shape[0]
    ga_blk0, gb_blk0 = 6 * D // tn, 7 * D // tn
    return pl.pallas_call(
        _merge_kernel,
        out_shape=jax.ShapeDtypeStruct((T, D), BF16),
        grid=(D // tn, T // tm),
        in_specs=[pl.BlockSpec((tm, D), lambda j, i: (i, 0)),
                  pl.BlockSpec((tm, D), lambda j, i: (i, 0)),
                  pl.BlockSpec((D, tn), lambda j, i: (0, j)),
                  pl.BlockSpec((D, tn), lambda j, i: (0, j)),
                  pl.BlockSpec((tm, tn), lambda j, i: (i, ga_blk0 + j)),
                  pl.BlockSpec((tm, tn), lambda j, i: (i, gb_blk0 + j))],
        out_specs=pl.BlockSpec((tm, tn), lambda j, i: (i, j)),
        compiler_params=_params(("arbitrary", "parallel")),
        name="merge",
    )(o_a, o_b, w_a, w_b, proj, proj)


def _out_kernel(m_ref, w_ref, x_ref, gate_ref, g_ref, scale_ref, shift_ref, x1_ref, h2_ref):
    x1_ref[...] = _dot(m_ref[...], w_ref[...])
    gate = gate_ref[...]
    gain = g_ref[...] * (1.0 + scale_ref[...])
    shift = shift_ref[...]

    def body(r):
        x1 = x_ref[r, :] + gate * x1_ref[r, :]
        x1_ref[r, :] = x1
        h2_ref[r, :] = _modulated_norm_rows(x1, gain, shift).astype(BF16)

    _for_row_slabs(x_ref.shape[0], body)


def _out_proj(merged, w_out, x2d, mod4, g2, S, tm=512):
    T, D = x2d.shape
    per_b = S // tm
    mod_spec = lambda k: pl.BlockSpec((None, None, 1, D), lambda i: (i // per_b, k, 0, 0))
    return pl.pallas_call(
        _out_kernel,
        out_shape=(jax.ShapeDtypeStruct((T, D), F32), jax.ShapeDtypeStruct((T, D), BF16)),
        grid=(T // tm,),
        in_specs=[pl.BlockSpec((tm, D), lambda i: (i, 0)),
                  pl.BlockSpec((D, D), lambda i: (0, 0)),
                  pl.BlockSpec((tm, D), lambda i: (i, 0)),
                  mod_spec(2),
                  pl.BlockSpec((1, D), lambda i: (0, 0)),
                  mod_spec(4),
                  mod_spec(3)],
        out_specs=(pl.BlockSpec((tm, D), lambda i: (i, 0)),
                   pl.BlockSpec((tm, D), lambda i: (i, 0))),
        compiler_params=_params(("parallel",)),
        name="out_proj",
    )(merged, w_out, x2d, mod4, g2, mod4, mod4)


def _ffn_kernel(h_ref, w1_ref, w2_ref, x1_ref, gate_ref, g_ref, o_ref, acc_ref):
    f = pl.program_id(1)

    @pl.when(f == 0)
    def _():
        acc_ref[...] = jnp.zeros_like(acc_ref)

    a = jnp.maximum(_dot(h_ref[...], w1_ref[...]), 0.0)
    acc_ref[...] += _dot((a * a).astype(BF16), w2_ref[...])

    @pl.when(f == pl.num_programs(1) - 1)
    def _():
        gate = gate_ref[...]
        g = g_ref[...]

        def body(r):
            x2 = x1_ref[r, :] + gate * acc_ref[r, :]
            o_ref[r, :] = x2 * lax.rsqrt(jnp.mean(x2 * x2, axis=-1, keepdims=True) + EPS) * g

        _for_row_slabs(x1_ref.shape[0], body)


def _ffn(h2, w1, w2, x1, mod4, g_final, S, tm=512, tf=1024):
    T, D = x1.shape
    F = w1.shape[1]
    per_b = S // tm
    return pl.pallas_call(
        _ffn_kernel,
        out_shape=jax.ShapeDtypeStruct((T, D), F32),
        grid=(T // tm, F // tf),
        in_specs=[pl.BlockSpec((tm, D), lambda i, f: (i, 0)),
                  pl.BlockSpec((D, tf), lambda i, f: (0, f)),
                  pl.BlockSpec((tf, D), lambda i, f: (f, 0)),
                  pl.BlockSpec((tm, D), lambda i, f: (i, 0)),
                  pl.BlockSpec((None, None, 1, D), lambda i, f: (i // per_b, 5, 0, 0)),
                  pl.BlockSpec((1, D), lambda i, f: (0, 0))],
        out_specs=pl.BlockSpec((tm, D), lambda i, f: (i, 0)),
        scratch_shapes=[pltpu.VMEM((tm, D), F32)],
        compiler_params=_params(("parallel", "arbitrary")),
        name="ffn",
    )(h2, w1, w2, x1, mod4, g_final)


def _pad_to(a, rows, cols):
    return jnp.pad(a, ((0, rows - a.shape[0]), (0, cols - a.shape[1])))


def kernel(x, c, w_ada, b_ada, g_norm1, w_in, w_gla_a_up, b_gla_a, g_gla_out, b_fox_f, w_proj_gla,
           w_proj_fox, w_out, g_norm2, w_ff1, w_ff2, g_final):
    B, S, D = x.shape
    assert w_ada.shape[0] == 1, "the final norm is fused into the last layer's MLP kernel"
    l = 0
    x2d = x.reshape(B * S, D)
    mod4 = _ada(c, w_ada[l], b_ada[l][None, :]).reshape(B, N_MOD, 1, D)

    proj, low = _in_proj(x2d, g_norm1[l][None, :], mod4, jnp.swapaxes(w_in[l], 0, 1), S)

    wup_pad = _pad_to(w_gla_a_up[l], LANES, D // 2).astype(BF16)
    o_a = _gla(proj, low, wup_pad, b_gla_a[l][None, :], g_gla_out[l][None, :], B, S, D)

    bf_pad = _pad_to(b_fox_f[l][None, :], 1, LANES)
    cum_col, cum_row = _fox_cum(low, bf_pad, B, S)
    o_b = _fox(proj, cum_col, cum_row, B, S, D)

    merged = _merge(o_a, o_b, w_proj_gla[l].astype(BF16), w_proj_fox[l].astype(BF16), proj, D)
    x1, h2 = _out_proj(merged, w_out[l].astype(BF16), x2d, mod4, g_norm2[l][None, :], S)
    out = _ffn(h2, w_ff1[l].astype(BF16), w_ff2[l].astype(BF16), x1, mod4, g_final[None, :], S)
    return out.reshape(B, S, D)
```

```python
import functools
import math

import jax
import jax.numpy as jnp
from jax import lax
from jax.experimental import pallas as pl
from jax.experimental.pallas import tpu as pltpu

F32 = jnp.float32
BF16 = jnp.bfloat16

EPS = 1e-6
LANES = 128
BF16_ROWS = 16
GLA_HEADS = 4
GLA_RANK = 16
GLA_TAU = 16.0
GLA_CHUNK = 64
FOX_HEADS = 16
N_MOD = 6
NEG_BIG = -1e30
LOG2E = math.log2(math.e)
SLAB = 16
SLAB_UNROLL = 4
VMEM_LIMIT = 56 * 1024 * 1024

_NT = (((1,), (1,)), ((), ()))
_TN = (((0,), (0,)), ((), ()))


def _dot(a, b):
    return jnp.dot(a, b, preferred_element_type=F32)


def _log_sigmoid(z):
    return jnp.minimum(z, 0.0) - jnp.log(1.0 + jnp.exp(-jnp.abs(z)))


def _silu(z):
    return z * jax.nn.sigmoid(z)


def _split_bf16(a):
    hi = a.astype(BF16)
    lo = (a - hi.astype(F32)).astype(BF16)
    return hi, lo


def _params(sem):
    return pltpu.CompilerParams(dimension_semantics=sem, vmem_limit_bytes=VMEM_LIMIT)


def _ada_kernel(c_ref, w_ref, b_ref, o_ref):
    cond = _silu(c_ref[...]).astype(BF16)
    o_ref[...] = _dot(cond, w_ref[...].astype(BF16)) + b_ref[...]


def _ada(c, w_ada, b_ada, tn=1024):
    B, D = c.shape
    N = w_ada.shape[1]
    return pl.pallas_call(
        _ada_kernel,
        out_shape=jax.ShapeDtypeStruct((B, N), F32),
        grid=(N // tn,),
        in_specs=[pl.BlockSpec((B, D), lambda j: (0, 0)),
                  pl.BlockSpec((D, tn), lambda j: (0, j)),
                  pl.BlockSpec((1, tn), lambda j: (0, j))],
        out_specs=pl.BlockSpec((B, tn), lambda j: (0, j)),
        compiler_params=_params(("arbitrary",)),
        name="ada",
    )(c, w_ada, b_ada)


def _modulated_norm_rows(x, gain, shift):
    return x * lax.rsqrt(jnp.mean(x * x, axis=-1, keepdims=True) + EPS) * gain + shift


def _for_row_slabs(n_rows, body):
    def step(s, carry):
        body(pl.ds(pl.multiple_of(s * SLAB, SLAB), SLAB))
        return carry

    lax.fori_loop(0, n_rows // SLAB, step, 0, unroll=SLAB_UNROLL)


def _in_kernel(x_ref, g_ref, scale_ref, shift_ref, wt_ref, wa_ref, wb_ref, proj_ref, low_ref, h_ref):
    @pl.when(pl.program_id(1) == 0)
    def _():
        gain = g_ref[...] * (1.0 + scale_ref[...])
        shift = shift_ref[...]

        def body(r):
            h_ref[r, :] = _modulated_norm_rows(x_ref[r, :], gain, shift).astype(BF16)

        _for_row_slabs(x_ref.shape[0], body)
        w_narrow = jnp.concatenate([wa_ref[...], wb_ref[...]], axis=0).astype(BF16)
        seg = lax.dot_general(h_ref[...], w_narrow, _NT, preferred_element_type=F32)
        lane = lax.broadcasted_iota(jnp.int32, seg.shape, 1)
        keep = (lane < GLA_RANK) | ((lane >= LANES) & (lane < LANES + FOX_HEADS))
        low_ref[...] = jnp.where(keep, seg, 0.0)

    w_t = wt_ref[...].astype(BF16)
    proj_ref[...] = lax.dot_general(h_ref[...], w_t, _NT, preferred_element_type=F32).astype(proj_ref.dtype)


def _in_proj(x2d, g1, mod4, w_t, S, tm=1024, tn=1024):
    T, D = x2d.shape
    N = 8 * D
    per_b = S // tm
    align = math.gcd(tn, GLA_RANK, FOX_HEADS)

    def w_rows(i, j):
        skip = jnp.where(j >= 3 * D // tn, GLA_RANK, 0) + jnp.where(j >= 6 * D // tn, FOX_HEADS, 0)
        return (pl.multiple_of(j * tn + skip, align), 0)

    narrow_spec = lambda row0: pl.BlockSpec((pl.Element(LANES), pl.Element(D)), lambda i, j: (row0, 0))
    return pl.pallas_call(
        _in_kernel,
        out_shape=(jax.ShapeDtypeStruct((T, N), BF16), jax.ShapeDtypeStruct((T, 2 * LANES), F32)),
        grid=(T // tm, N // tn),
        in_specs=[pl.BlockSpec((tm, D), lambda i, j: (i, 0)),
                  pl.BlockSpec((1, D), lambda i, j: (0, 0)),
                  pl.BlockSpec((None, None, 1, D), lambda i, j: (i // per_b, 1, 0, 0)),
                  pl.BlockSpec((None, None, 1, D), lambda i, j: (i // per_b, 0, 0, 0)),
                  pl.BlockSpec((pl.Element(tn), pl.Element(D)), w_rows),
                  narrow_spec(3 * D),
                  narrow_spec(6 * D + GLA_RANK)],
        out_specs=(pl.BlockSpec((tm, tn), lambda i, j: (i, j)),
                   pl.BlockSpec((tm, 2 * LANES), lambda i, j: (i, 0))),
        scratch_shapes=[pltpu.VMEM((tm, D), BF16)],
        compiler_params=_params(("parallel", "arbitrary")),
        name="in_proj",
    )(x2d, g1, mod4, mod4, w_t, w_t, w_t)


def _gla_kernel(q_ref, k_ref, v_ref, gg_ref, low_ref, wup_ref, ba_ref, g_ref, o_ref, *, q_scale, chunk):
    C, SB = chunk, GLA_CHUNK
    S, dk = q_ref.shape
    dv = v_ref.shape[1]
    tril = (lax.broadcasted_iota(jnp.int32, (C, C), 1) <= lax.broadcasted_iota(jnp.int32, (C, C), 0)).astype(BF16)
    wup = wup_ref[...]
    ba = ba_ref[...]
    g = g_ref[...]
    state_t = jnp.zeros((dv, dk), F32)
    for c in range(S // C):
        rows = slice(c * C, (c + 1) * C)
        q = q_ref[rows, :].astype(F32) * q_scale
        k = k_ref[rows, :].astype(F32)
        v = v_ref[rows, :]
        z = _dot(low_ref[rows, :].astype(BF16), wup) + ba
        la = _log_sigmoid(z) * (1.0 / GLA_TAU)
        la_hi, la_lo = _split_bf16(la)
        b = _dot(tril, la_hi) + _dot(tril, la_lo)
        b_last = b[C - 1:C, :]
        q_dec = (q * jnp.exp(b)).astype(BF16)
        k_dec = (k * jnp.exp(b_last - b)).astype(BF16)

        attn_rows = []
        for i in range(C // SB):
            sub = slice(i * SB, (i + 1) * SB)
            n_keys = -(-(i + 1) * SB // LANES) * LANES
            ref = b[i * SB - 1:i * SB, :] if i > 0 else jnp.zeros((1, dk), F32)
            q_loc = (q[sub] * jnp.exp(b[sub] - ref)).astype(BF16)
            k_loc = (k[:n_keys] * jnp.exp(ref - b[:n_keys])).astype(BF16)
            a = lax.dot_general(q_loc, k_loc, _NT, preferred_element_type=F32)
            key_i = lax.broadcasted_iota(jnp.int32, (SB, n_keys), 1)
            qry_i = lax.broadcasted_iota(jnp.int32, (SB, n_keys), 0) + i * SB
            a = jnp.where(key_i <= qry_i, a, 0.0)
            if n_keys < C:
                a = jnp.concatenate([a, jnp.zeros((SB, C - n_keys), F32)], axis=1)
            attn_rows.append(a)
        attn = jnp.concatenate(attn_rows, axis=0).astype(BF16)

        o = _dot(attn, v) + lax.dot_general(q_dec, state_t.astype(BF16), _NT, preferred_element_type=F32)
        state_t = state_t * jnp.exp(b_last) + lax.dot_general(v, k_dec, _TN, preferred_element_type=F32)
        y = o * lax.rsqrt(jnp.mean(o * o, axis=-1, keepdims=True) + EPS) * g
        o_ref[rows, :] = (y * _silu(gg_ref[rows, :].astype(F32))).astype(o_ref.dtype)


def _gla(proj, low, wup_pad, b_a, g_out, B, S, D, chunk=256):
    T = B * S
    H = GLA_HEADS
    dk = (D // 2) // H
    dv = D // H
    k_blk0 = (D // 2) // dk
    v_blk0 = D // dv
    gg_blk0 = 2 * D // dv
    return pl.pallas_call(
        functools.partial(_gla_kernel, q_scale=dk ** -0.5, chunk=chunk),
        out_shape=jax.ShapeDtypeStruct((T, D), BF16),
        grid=(B, H),
        in_specs=[pl.BlockSpec((S, dk), lambda b, h: (b, h)),
                  pl.BlockSpec((S, dk), lambda b, h: (b, k_blk0 + h)),
                  pl.BlockSpec((S, dv), lambda b, h: (b, v_blk0 + h)),
                  pl.BlockSpec((S, dv), lambda b, h: (b, gg_blk0 + h)),
                  pl.BlockSpec((S, LANES), lambda b, h: (b, 0)),
                  pl.BlockSpec((LANES, dk), lambda b, h: (0, h)),
                  pl.BlockSpec((1, dk), lambda b, h: (0, h)),
                  pl.BlockSpec((1, dv), lambda b, h: (0, 0))],
        out_specs=pl.BlockSpec((S, dv), lambda b, h: (b, h)),
        compiler_params=_params(("parallel", "arbitrary")),
        name="gla",
    )(proj, proj, proj, proj, low, wup_pad, b_a, g_out)


def _cum_kernel(low_ref, bf_ref, col_ref, row_ref, *, blk):
    S = low_ref.shape[0]
    n_heads = row_ref.shape[0]
    r_i = lax.broadcasted_iota(jnp.int32, (blk, blk), 0)
    c_i = lax.broadcasted_iota(jnp.int32, (blk, blk), 1)
    tril = (c_i <= r_i).astype(BF16)
    triu = (r_i <= c_i).astype(BF16)
    bf = bf_ref[...]
    carry_row = jnp.zeros((1, LANES), F32)
    carry_col = jnp.zeros((LANES, 1), F32)
    for i in range(S // blk):
        r = slice(i * blk, (i + 1) * blk)
        lf = _log_sigmoid(low_ref[r, :] + bf)
        hi, lo = _split_bf16(lf)
        loc = _dot(tril, hi) + _dot(tril, lo) + carry_row
        col_ref[r, :] = loc
        carry_row = loc[blk - 1:blk, :]
        loc_t = (lax.dot_general(hi, triu, _TN, preferred_element_type=F32)
                 + lax.dot_general(lo, triu, _TN, preferred_element_type=F32) + carry_col)
        row_ref[:, r] = loc_t[:n_heads, :]
        carry_col = loc_t[:, blk - 1:blk]


def _fox_cum(low, bf_pad, B, S, blk=128):
    return pl.pallas_call(
        functools.partial(_cum_kernel, blk=blk),
        out_shape=(jax.ShapeDtypeStruct((B, S, LANES), F32),
                   jax.ShapeDtypeStruct((B, FOX_HEADS, S), F32)),
        grid=(B,),
        in_specs=[pl.BlockSpec((S, LANES), lambda b: (b, 1)),
                  pl.BlockSpec((1, LANES), lambda b: (0, 0))],
        out_specs=(pl.BlockSpec((None, S, LANES), lambda b: (b, 0, 0)),
                   pl.BlockSpec((None, FOX_HEADS, S), lambda b: (b, 0, 0))),
        compiler_params=_params(("arbitrary",)),
        name="fox_cum",
    )(low, bf_pad)


def _fox_kernel(q_ref, k_ref, v_ref, cc_ref, cr_ref, o_ref, qs_ref, m_ref, l_ref, acc_ref, *, blk, hd, scale):
    S = q_ref.shape[0]
    n_heads = q_ref.shape[1] // hd
    lane = lax.broadcasted_iota(jnp.int32, (S, LANES), 1)
    key_i = lax.broadcasted_iota(jnp.int32, (blk, blk), 0)
    qry_i = lax.broadcasted_iota(jnp.int32, (blk, blk), 1)
    causal = key_i <= qry_i
    ones_rows = (lax.broadcasted_iota(jnp.int32, (BF16_ROWS, blk), 0) == 0).astype(BF16)

    for g in range(n_heads):
        h = pl.program_id(1) * n_heads + g
        cols = slice(g * hd, (g + 1) * hd)
        qs_ref[g] = (q_ref[:, cols].astype(F32) * (scale * LOG2E)).astype(BF16)
        ck_all = jnp.sum(jnp.where(lane == h, cc_ref[...], 0.0), axis=-1, keepdims=True) * LOG2E
        cq_all = cr_ref[pl.ds(h, 1), :] * LOG2E
        for j in range(S // blk):
            r0 = j * blk
            ks = slice(r0, r0 + blk)
            t = lax.dot_general(k_ref[ks, cols], qs_ref[g, r0:, :], _NT, preferred_element_type=F32) - ck_all[ks]
            t_diag = jnp.where(causal, t[:, :blk], NEG_BIG)
            t = t_diag if r0 + blk == S else jnp.concatenate([t_diag, t[:, blk:]], axis=1)
            cq = cq_all[:, r0:]
            t_max = jnp.max(t, axis=0, keepdims=True) + cq
            if j == 0:
                m_new = t_max
            else:
                m_old = m_ref[g, :, r0:]
                m_new = jnp.maximum(m_old, t_max)
                alpha = jnp.exp2(m_old - m_new)
            p = jnp.exp2(t - (m_new - cq))
            pv_l = _dot(jnp.concatenate([v_ref[ks, cols].T, ones_rows], axis=0), p.astype(BF16))
            pv, p_sum = pv_l[:hd], pv_l[hd:hd + 1]
            if j == 0:
                l_ref[g] = p_sum
                acc_ref[g] = pv
            else:
                l_ref[g, :, r0:] = alpha * l_ref[g, :, r0:] + p_sum
                acc_ref[g, :, r0:] = alpha * acc_ref[g, :, r0:] + pv
            m_ref[g, :, r0:] = m_new
        o_ref[:, cols] = (acc_ref[g] / l_ref[g]).T.astype(o_ref.dtype)


def _fox(proj, cum_col, cum_row, B, S, D, blk=256, heads_per_step=4):
    T = B * S
    H = FOX_HEADS
    hd = D // H
    G = heads_per_step
    w = G * hd
    q_blk0, k_blk0, v_blk0 = 3 * D // w, 4 * D // w, 5 * D // w
    return pl.pallas_call(
        functools.partial(_fox_kernel, blk=blk, hd=hd, scale=1.0 / math.sqrt(hd)),
        out_shape=jax.ShapeDtypeStruct((T, D), BF16),
        grid=(B, H // G),
        in_specs=[pl.BlockSpec((S, w), lambda b, h: (b, q_blk0 + h)),
                  pl.BlockSpec((S, w), lambda b, h: (b, k_blk0 + h)),
                  pl.BlockSpec((S, w), lambda b, h: (b, v_blk0 + h)),
                  pl.BlockSpec((None, S, LANES), lambda b, h: (b, 0, 0)),
                  pl.BlockSpec((None, H, S), lambda b, h: (b, 0, 0))],
        out_specs=pl.BlockSpec((S, w), lambda b, h: (b, h)),
        scratch_shapes=[pltpu.VMEM((G, S, hd), BF16), pltpu.VMEM((G, 1, S), F32), pltpu.VMEM((G, 1, S), F32),
                        pltpu.VMEM((G, hd, S), F32)],
        compiler_params=_params(("parallel", "arbitrary")),
        name="fox",
    )(proj, proj, proj, cum_col, cum_row)


def _merge_kernel(oa_ref, ob_ref, wa_ref, wb_ref, ga_ref, gb_ref, o_ref):
    a = _dot(oa_ref[...], wa_ref[...])
    b = _dot(ob_ref[...], wb_ref[...])
    merged = jax.nn.sigmoid(ga_ref[...].astype(F32)) * a + jax.nn.sigmoid(gb_ref[...].astype(F32)) * b
    o_ref[...] = merged.astype(o_ref.dtype)


def _merge(o_a, o_b, w_a, w_b, proj, D, tm=512, tn=1024):
    T = o_a.shape[0]
    ga_blk0, gb_blk0 = 6 * D // tn, 7 * D // tn
    return pl.pallas_call(
        _merge_kernel,
        out_shape=jax.ShapeDtypeStruct((T, D), BF16),
        grid=(D // tn, T // tm),
        in_specs=[pl.BlockSpec((tm, D), lambda j, i: (i, 0)),
                  pl.BlockSpec((tm, D), lambda j, i: (i, 0)),
                  pl.BlockSpec((D, tn), lambda j, i: (0, j)),
                  pl.BlockSpec((D, tn), lambda j, i: (0, j)),
                  pl.BlockSpec((tm, tn), lambda j, i: (i, ga_blk0 + j)),
                  pl.BlockSpec((tm, tn), lambda j, i: (i, gb_blk0 + j))],
        out_specs=pl.BlockSpec((tm, tn), lambda j, i: (i, j)),
        compiler_params=_params(("arbitrary", "arbitrary")),
        name="merge",
    )(o_a, o_b, w_a, w_b, proj, proj)


def _out_kernel(m_ref, w_ref, x_ref, gate_ref, g_ref, scale_ref, shift_ref, x1_ref, h2_ref, raw_a, raw_b):
    i = pl.program_id(0)

    @pl.when(i == 0)
    def _():
        raw_b[...] = jnp.zeros_like(raw_b)

    gate = gate_ref[...]
    gain = g_ref[...] * (1.0 + scale_ref[...])
    shift = shift_ref[...]

    def step(cur, prev):
        cur[...] = _dot(m_ref[...], w_ref[...])
        for s in range(x_ref.shape[0] // SLAB):
            r = slice(s * SLAB, (s + 1) * SLAB)
            x1 = x_ref[r, :] + gate * prev[r, :]
            x1_ref[r, :] = x1
            h2_ref[r, :] = _modulated_norm_rows(x1, gain, shift).astype(BF16)

    @pl.when(i % 2 == 0)
    def _():
        step(raw_a, raw_b)

    @pl.when(i % 2 == 1)
    def _():
        step(raw_b, raw_a)


def _out_proj(merged, w_out, x2d, mod4, g2, S, tm=512):
    T, D = x2d.shape
    per_b = S // tm
    n = T // tm
    done = lambda i: jnp.maximum(i - 1, 0)
    mod_spec = lambda k: pl.BlockSpec((None, None, 1, D), lambda i: (done(i) // per_b, k, 0, 0))
    return pl.pallas_call(
        _out_kernel,
        out_shape=(jax.ShapeDtypeStruct((T, D), F32), jax.ShapeDtypeStruct((T, D), BF16)),
        grid=(n + 1,),
        in_specs=[pl.BlockSpec((tm, D), lambda i: (jnp.minimum(i, n - 1), 0)),
                  pl.BlockSpec((D, D), lambda i: (0, 0)),
                  pl.BlockSpec((tm, D), lambda i: (done(i), 0)),
                  mod_spec(2),
                  pl.BlockSpec((1, D), lambda i: (0, 0)),
                  mod_spec(4),
                  mod_spec(3)],
        out_specs=(pl.BlockSpec((tm, D), lambda i: (done(i), 0)),
                   pl.BlockSpec((tm, D), lambda i: (done(i), 0))),
        scratch_shapes=[pltpu.VMEM((tm, D), F32), pltpu.VMEM((tm, D), F32)],
        compiler_params=_params(("arbitrary",)),
        name="out_proj",
    )(merged, w_out, x2d, mod4, g2, mod4, mod4)


def _ffn_kernel(h_ref, w1_ref, w2_ref, x1_ref, gate_ref, g_ref, o_ref, acc_ref):
    f = pl.program_id(1)

    @pl.when(f == 0)
    def _():
        acc_ref[...] = jnp.zeros_like(acc_ref)

    a = jnp.maximum(_dot(h_ref[...], w1_ref[...]), 0.0)
    acc_ref[...] += _dot((a * a).astype(BF16), w2_ref[...])

    @pl.when(f == pl.num_programs(1) - 1)
    def _():
        gate = gate_ref[...]
        g = g_ref[...]

        def body(r):
            x2 = x1_ref[r, :] + gate * acc_ref[r, :]
            o_ref[r, :] = x2 * lax.rsqrt(jnp.mean(x2 * x2, axis=-1, keepdims=True) + EPS) * g

        _for_row_slabs(x1_ref.shape[0], body)


def _ffn(h2, w1, w2, x1, mod4, g_final, S, tm=512, tf=1024):
    T, D = x1.shape
    F = w1.shape[1]
    per_b = S // tm
    return pl.pallas_call(
        _ffn_kernel,
        out_shape=jax.ShapeDtypeStruct((T, D), F32),
        grid=(T // tm, F // tf),
        in_specs=[pl.BlockSpec((tm, D), lambda i, f: (i, 0)),
                  pl.BlockSpec((D, tf), lambda i, f: (0, f)),
                  pl.BlockSpec((tf, D), lambda i, f: (f, 0)),
                  pl.BlockSpec((tm, D), lambda i, f: (i, 0)),
                  pl.BlockSpec((None, None, 1, D), lambda i, f: (i // per_b, 5, 0, 0)),
                  pl.BlockSpec((1, D), lambda i, f: (0, 0))],
        out_specs=pl.BlockSpec((tm, D), lambda i, f: (i, 0)),
        scratch_shapes=[pltpu.VMEM((tm, D), F32)],
        compiler_params=_params(("parallel", "arbitrary")),
        name="ffn",
    )(h2, w1, w2, x1, mod4, g_final)


def _pad_to(a, rows, cols):
    return jnp.pad(a, ((0, rows - a.shape[0]), (0, cols - a.shape[1])))


def kernel(x, c, w_ada, b_ada, g_norm1, w_in, w_gla_a_up, b_gla_a, g_gla_out, b_fox_f, w_proj_gla,
           w_proj_fox, w_out, g_norm2, w_ff1, w_ff2, g_final):
    B, S, D = x.shape
    assert w_ada.shape[0] == 1, "the final norm is fused into the last layer's MLP kernel"
    l = 0
    x2d = x.reshape(B * S, D)
    mod4 = _ada(c, w_ada[l], b_ada[l][None, :]).reshape(B, N_MOD, 1, D)

    proj, low = _in_proj(x2d, g_norm1[l][None, :], mod4, jnp.swapaxes(w_in[l], 0, 1), S)

    wup_pad = _pad_to(w_gla_a_up[l], LANES, D // 2).astype(BF16)
    o_a = _gla(proj, low, wup_pad, b_gla_a[l][None, :], g_gla_out[l][None, :], B, S, D)

    bf_pad = _pad_to(b_fox_f[l][None, :], 1, LANES)
    cum_col, cum_row = _fox_cum(low, bf_pad, B, S)
    o_b = _fox(proj, cum_col, cum_row, B, S, D)

    merged = _merge(o_a, o_b, w_proj_gla[l].astype(BF16), w_proj_fox[l].astype(BF16), proj, D)
    x1, h2 = _out_proj(merged, w_out[l].astype(BF16), x2d, mod4, g_norm2[l][None, :], S)
    out = _ffn(h2, w_ff1[l].astype(BF16), w_ff2[l].astype(BF16), x1, mod4, g_final[None, :], S)
    return out.reshape(B, S, D)
```

```python
import functools
import math

import jax
import jax.numpy as jnp
from jax import lax
from jax.experimental import pallas as pl
from jax.experimental.pallas import tpu as pltpu

F32 = jnp.float32
BF16 = jnp.bfloat16

EPS = 1e-6
LANES = 128
BF16_ROWS = 16
GLA_HEADS = 4
GLA_RANK = 16
GLA_TAU = 16.0
GLA_CHUNK = 64
FOX_HEADS = 16
N_MOD = 6
NEG_BIG = -1e30
LOG2E = math.log2(math.e)
SLAB = 16
SLAB_UNROLL = 4
VMEM_LIMIT = 56 * 1024 * 1024

_NT = (((1,), (1,)), ((), ()))
_TN = (((0,), (0,)), ((), ()))


def _dot(a, b):
    return jnp.dot(a, b, preferred_element_type=F32)


def _log_sigmoid(z):
    return jnp.minimum(z, 0.0) - jnp.log(1.0 + jnp.exp(-jnp.abs(z)))


def _silu(z):
    return z * jax.nn.sigmoid(z)


def _split_bf16(a):
    hi = a.astype(BF16)
    lo = (a - hi.astype(F32)).astype(BF16)
    return hi, lo


def _params(sem):
    return pltpu.CompilerParams(dimension_semantics=sem, vmem_limit_bytes=VMEM_LIMIT)


def _ada_kernel(c_ref, w_ref, b_ref, o_ref):
    cond = _silu(c_ref[...]).astype(BF16)
    o_ref[...] = _dot(cond, w_ref[...].astype(BF16)) + b_ref[...]


def _ada(c, w_ada, b_ada, tn=1024):
    B, D = c.shape
    N = w_ada.shape[1]
    return pl.pallas_call(
        _ada_kernel,
        out_shape=jax.ShapeDtypeStruct((B, N), F32),
        grid=(N // tn,),
        in_specs=[pl.BlockSpec((B, D), lambda j: (0, 0)),
                  pl.BlockSpec((D, tn), lambda j: (0, j)),
                  pl.BlockSpec((1, tn), lambda j: (0, j))],
        out_specs=pl.BlockSpec((B, tn), lambda j: (0, j)),
        compiler_params=_params(("arbitrary",)),
        name="ada",
    )(c, w_ada, b_ada)


def _modulated_norm_rows(x, gain, shift):
    return x * lax.rsqrt(jnp.mean(x * x, axis=-1, keepdims=True) + EPS) * gain + shift


def _for_row_slabs(n_rows, body):
    def step(s, carry):
        body(pl.ds(pl.multiple_of(s * SLAB, SLAB), SLAB))
        return carry

    lax.fori_loop(0, n_rows // SLAB, step, 0, unroll=SLAB_UNROLL)


def _in_kernel(x_ref, g_ref, scale_ref, shift_ref, wt_ref, wa_ref, wb_ref, proj_ref, low_ref, h_ref):
    @pl.when(pl.program_id(1) == 0)
    def _():
        gain = g_ref[...] * (1.0 + scale_ref[...])
        shift = shift_ref[...]

        def body(r):
            h_ref[r, :] = _modulated_norm_rows(x_ref[r, :], gain, shift).astype(BF16)

        _for_row_slabs(x_ref.shape[0], body)
        w_narrow = jnp.concatenate([wa_ref[...], wb_ref[...]], axis=0).astype(BF16)
        seg = lax.dot_general(h_ref[...], w_narrow, _NT, preferred_element_type=F32)
        lane = lax.broadcasted_iota(jnp.int32, seg.shape, 1)
        keep = (lane < GLA_RANK) | ((lane >= LANES) & (lane < LANES + FOX_HEADS))
        low_ref[...] = jnp.where(keep, seg, 0.0)

    w_t = wt_ref[...].astype(BF16)
    proj_ref[...] = lax.dot_general(h_ref[...], w_t, _NT, preferred_element_type=F32).astype(proj_ref.dtype)


def _in_proj(x2d, g1, mod4, w_t, S, tm=1024, tn=1024):
    T, D = x2d.shape
    N = 8 * D
    per_b = S // tm
    align = math.gcd(tn, GLA_RANK, FOX_HEADS)

    def w_rows(i, j):
        skip = jnp.where(j >= 3 * D // tn, GLA_RANK, 0) + jnp.where(j >= 6 * D // tn, FOX_HEADS, 0)
        return (pl.multiple_of(j * tn + skip, align), 0)

    narrow_spec = lambda row0: pl.BlockSpec((pl.Element(LANES), pl.Element(D)), lambda i, j: (row0, 0))
    return pl.pallas_call(
        _in_kernel,
        out_shape=(jax.ShapeDtypeStruct((T, N), BF16), jax.ShapeDtypeStruct((T, 2 * LANES), F32)),
        grid=(T // tm, N // tn),
        in_specs=[pl.BlockSpec((tm, D), lambda i, j: (i, 0)),
                  pl.BlockSpec((1, D), lambda i, j: (0, 0)),
                  pl.BlockSpec((None, None, 1, D), lambda i, j: (i // per_b, 1, 0, 0)),
                  pl.BlockSpec((None, None, 1, D), lambda i, j: (i // per_b, 0, 0, 0)),
                  pl.BlockSpec((pl.Element(tn), pl.Element(D)), w_rows),
                  narrow_spec(3 * D),
                  narrow_spec(6 * D + GLA_RANK)],
        out_specs=(pl.BlockSpec((tm, tn), lambda i, j: (i, j)),
                   pl.BlockSpec((tm, 2 * LANES), lambda i, j: (i, 0))),
        scratch_shapes=[pltpu.VMEM((tm, D), BF16)],
        compiler_params=_params(("parallel", "arbitrary")),
        name="in_proj",
    )(x2d, g1, mod4, mod4, w_t, w_t, w_t)


def _gla_kernel(q_ref, k_ref, v_ref, gg_ref, low_ref, wup_ref, ba_ref, g_ref, o_ref, *, q_scale, chunk):
    C, SB = chunk, GLA_CHUNK
    S, dk = q_ref.shape
    dv = v_ref.shape[1]
    tril = (lax.broadcasted_iota(jnp.int32, (C, C), 1) <= lax.broadcasted_iota(jnp.int32, (C, C), 0)).astype(BF16)
    wup = wup_ref[...]
    ba = ba_ref[...]
    g = g_ref[...]
    state_t = jnp.zeros((dv, dk), F32)
    for c in range(S // C):
        rows = slice(c * C, (c + 1) * C)
        q = q_ref[rows, :].astype(F32) * q_scale
        k = k_ref[rows, :].astype(F32)
        v = v_ref[rows, :]
        z = _dot(low_ref[rows, :].astype(BF16), wup) + ba
        la = _log_sigmoid(z) * (1.0 / GLA_TAU)
        la_hi, la_lo = _split_bf16(la)
        b = _dot(tril, la_hi) + _dot(tril, la_lo)
        b_last = b[C - 1:C, :]
        q_dec = (q * jnp.exp(b)).astype(BF16)
        k_dec = (k * jnp.exp(b_last - b)).astype(BF16)

        attn_rows = []
        for i in range(C // SB):
            sub = slice(i * SB, (i + 1) * SB)
            n_keys = -(-(i + 1) * SB // LANES) * LANES
            ref = b[i * SB - 1:i * SB, :] if i > 0 else jnp.zeros((1, dk), F32)
            q_loc = (q[sub] * jnp.exp(b[sub] - ref)).astype(BF16)
            k_loc = (k[:n_keys] * jnp.exp(ref - b[:n_keys])).astype(BF16)
            a = lax.dot_general(q_loc, k_loc, _NT, preferred_element_type=F32)
            key_i = lax.broadcasted_iota(jnp.int32, (SB, n_keys), 1)
            qry_i = lax.broadcasted_iota(jnp.int32, (SB, n_keys), 0) + i * SB
            a = jnp.where(key_i <= qry_i, a, 0.0)
            if n_keys < C:
                a = jnp.concatenate([a, jnp.zeros((SB, C - n_keys), F32)], axis=1)
            attn_rows.append(a)
        attn = jnp.concatenate(attn_rows, axis=0).astype(BF16)

        o = _dot(attn, v) + lax.dot_general(q_dec, state_t.astype(BF16), _NT, preferred_element_type=F32)
        state_t = state_t * jnp.exp(b_last) + lax.dot_general(v, k_dec, _TN, preferred_element_type=F32)
        y = o * lax.rsqrt(jnp.mean(o * o, axis=-1, keepdims=True) + EPS) * g
        o_ref[rows, :] = (y * _silu(gg_ref[rows, :].astype(F32))).astype(o_ref.dtype)


def _gla(proj, low, wup_pad, b_a, g_out, B, S, D, chunk=256):
    T = B * S
    H = GLA_HEADS
    dk = (D // 2) // H
    dv = D // H
    k_blk0 = (D // 2) // dk
    v_blk0 = D // dv
    gg_blk0 = 2 * D // dv
    return pl.pallas_call(
        functools.partial(_gla_kernel, q_scale=dk ** -0.5, chunk=chunk),
        out_shape=jax.ShapeDtypeStruct((T, D), BF16),
        grid=(B, H),
        in_specs=[pl.BlockSpec((S, dk), lambda b, h: (b, h)),
                  pl.BlockSpec((S, dk), lambda b, h: (b, k_blk0 + h)),
                  pl.BlockSpec((S, dv), lambda b, h: (b, v_blk0 + h)),
                  pl.BlockSpec((S, dv), lambda b, h: (b, gg_blk0 + h)),
                  pl.BlockSpec((S, LANES), lambda b, h: (b, 0)),
                  pl.BlockSpec((LANES, dk), lambda b, h: (0, h)),
                  pl.BlockSpec((1, dk), lambda b, h: (0, h)),
                  pl.BlockSpec((1, dv), lambda b, h: (0, 0))],
        out_specs=pl.BlockSpec((S, dv), lambda b, h: (b, h)),
        compiler_params=_params(("parallel", "arbitrary")),
        name="gla",
    )(proj, proj, proj, proj, low, wup_pad, b_a, g_out)


def _cum_kernel(low_ref, bf_ref, col_ref, row_ref, *, blk):
    S = low_ref.shape[0]
    n_heads = row_ref.shape[0]
    r_i = lax.broadcasted_iota(jnp.int32, (blk, blk), 0)
    c_i = lax.broadcasted_iota(jnp.int32, (blk, blk), 1)
    tril = (c_i <= r_i).astype(BF16)
    triu = (r_i <= c_i).astype(BF16)
    bf = bf_ref[...]
    carry_row = jnp.zeros((1, LANES), F32)
    carry_col = jnp.zeros((LANES, 1), F32)
    for i in range(S // blk):
        r = slice(i * blk, (i + 1) * blk)
        lf = _log_sigmoid(low_ref[r, :] + bf)
        hi, lo = _split_bf16(lf)
        loc = _dot(tril, hi) + _dot(tril, lo) + carry_row
        col_ref[r, :] = loc
        carry_row = loc[blk - 1:blk, :]
        loc_t = (lax.dot_general(hi, triu, _TN, preferred_element_type=F32)
                 + lax.dot_general(lo, triu, _TN, preferred_element_type=F32) + carry_col)
        row_ref[:, r] = loc_t[:n_heads, :]
        carry_col = loc_t[:, blk - 1:blk]


def _fox_cum(low, bf_pad, B, S, blk=128):
    return pl.pallas_call(
        functools.partial(_cum_kernel, blk=blk),
        out_shape=(jax.ShapeDtypeStruct((B, S, LANES), F32),
                   jax.ShapeDtypeStruct((B, FOX_HEADS, S), F32)),
        grid=(B,),
        in_specs=[pl.BlockSpec((S, LANES), lambda b: (b, 1)),
                  pl.BlockSpec((1, LANES), lambda b: (0, 0))],
        out_specs=(pl.BlockSpec((None, S, LANES), lambda b: (b, 0, 0)),
                   pl.BlockSpec((None, FOX_HEADS, S), lambda b: (b, 0, 0))),
        compiler_params=_params(("arbitrary",)),
        name="fox_cum",
    )(low, bf_pad)


def _fox_kernel(q_ref, k_ref, v_ref, cc_ref, cr_ref, o_ref, qs_ref, m_ref, l_ref, acc_ref, *, blk, hd, scale):
    S = q_ref.shape[0]
    n_heads = q_ref.shape[1] // hd
    lane = lax.broadcasted_iota(jnp.int32, (S, LANES), 1)
    key_i = lax.broadcasted_iota(jnp.int32, (blk, blk), 0)
    qry_i = lax.broadcasted_iota(jnp.int32, (blk, blk), 1)
    causal = key_i <= qry_i
    ones_rows = (lax.broadcasted_iota(jnp.int32, (BF16_ROWS, blk), 0) == 0).astype(BF16)

    for g in range(n_heads):
        h = pl.program_id(1) * n_heads + g
        cols = slice(g * hd, (g + 1) * hd)
        qs_ref[g] = (q_ref[:, cols].astype(F32) * (scale * LOG2E)).astype(BF16)
        ck_all = jnp.sum(jnp.where(lane == h, cc_ref[...], 0.0), axis=-1, keepdims=True) * LOG2E
        cq_all = cr_ref[pl.ds(h, 1), :] * LOG2E
        for j in range(S // blk):
            r0 = j * blk
            ks = slice(r0, r0 + blk)
            t = lax.dot_general(k_ref[ks, cols], qs_ref[g, r0:, :], _NT, preferred_element_type=F32) - ck_all[ks]
            t_diag = jnp.where(causal, t[:, :blk], NEG_BIG)
            t = t_diag if r0 + blk == S else jnp.concatenate([t_diag, t[:, blk:]], axis=1)
            cq = cq_all[:, r0:]
            t_max = jnp.max(t, axis=0, keepdims=True) + cq
            if j == 0:
                m_new = t_max
            else:
                m_old = m_ref[g, :, r0:]
                m_new = jnp.maximum(m_old, t_max)
                alpha = jnp.exp2(m_old - m_new)
            p = jnp.exp2(t - (m_new - cq))
            pv_l = _dot(jnp.concatenate([v_ref[ks, cols].T, ones_rows], axis=0), p.astype(BF16))
            pv, p_sum = pv_l[:hd], pv_l[hd:hd + 1]
            if j == 0:
                l_ref[g] = p_sum
                acc_ref[g] = pv
            else:
                l_ref[g, :, r0:] = alpha * l_ref[g, :, r0:] + p_sum
                acc_ref[g, :, r0:] = alpha * acc_ref[g, :, r0:] + pv
            m_ref[g, :, r0:] = m_new
        o_ref[:, cols] = (acc_ref[g] / l_ref[g]).T.astype(o_ref.dtype)


def _fox(proj, cum_col, cum_row, B, S, D, blk=256, heads_per_step=4):
    T = B * S
    H = FOX_HEADS
    hd = D // H
    G = heads_per_step
    w = G * hd
    q_blk0, k_blk0, v_blk0 = 3 * D // w, 4 * D // w, 5 * D // w
    return pl.pallas_call(
        functools.partial(_fox_kernel, blk=blk, hd=hd, scale=1.0 / math.sqrt(hd)),
        out_shape=jax.ShapeDtypeStruct((T, D), BF16),
        grid=(B, H // G),
        in_specs=[pl.BlockSpec((S, w), lambda b, h: (b, q_blk0 + h)),
                  pl.BlockSpec((S, w), lambda b, h: (b, k_blk0 + h)),
                  pl.BlockSpec((S, w), lambda b, h: (b, v_blk0 + h)),
                  pl.BlockSpec((None, S, LANES), lambda b, h: (b, 0, 0)),
                  pl.BlockSpec((None, H, S), lambda b, h: (b, 0, 0))],
        out_specs=pl.BlockSpec((S, w), lambda b, h: (b, h)),
        scratch_shapes=[pltpu.VMEM((G, S, hd), BF16), pltpu.VMEM((G, 1, S), F32), pltpu.VMEM((G, 1, S), F32),
                        pltpu.VMEM((G, hd, S), F32)],
        compiler_params=_params(("parallel", "arbitrary")),
        name="fox",
    )(proj, proj, proj, cum_col, cum_row)


def _merge_kernel(oa_ref, ob_ref, wa_ref, wb_ref, ga_ref, gb_ref, o_ref):
    a = _dot(oa_ref[...], wa_ref[...])
    b = _dot(ob_ref[...], wb_ref[...])
    merged = jax.nn.sigmoid(ga_ref[...].astype(F32)) * a + jax.nn.sigmoid(gb_ref[...].astype(F32)) * b
    o_ref[...] = merged.astype(o_ref.dtype)


def _merge(o_a, o_b, w_a, w_b, proj, D, tm=512, tn=1024):
    T = o_a.shape[0]
    ga_blk0, gb_blk0 = 6 * D // tn, 7 * D // tn
    return pl.pallas_call(
        _merge_kernel,
        out_shape=jax.ShapeDtypeStruct((T, D), BF16),
        grid=(D // tn, T // tm),
        in_specs=[pl.BlockSpec((tm, D), lambda j, i: (i, 0)),
                  pl.BlockSpec((tm, D), lambda j, i: (i, 0)),
                  pl.BlockSpec((D, tn), lambda j, i: (0, j)),
                  pl.BlockSpec((D, tn), lambda j, i: (0, j)),
                  pl.BlockSpec((tm, tn), lambda j, i: (i, ga_blk0 + j)),
                  pl.BlockSpec((tm, tn), lambda j, i: (i, gb_blk0 + j))],
        out_specs=pl.BlockSpec((tm, tn), lambda j, i: (i, j)),
        compiler_params=_params(("arbitrary", "arbitrary")),
        name="merge",
    )(o_a, o_b, w_a, w_b, proj, proj)


def _out_kernel(m_ref, w_ref, x_ref, gate_ref, g_ref, scale_ref, shift_ref, x1_ref, h2_ref, raw_a, raw_b):
    i = pl.program_id(0)

    @pl.when(i == 0)
    def _():
        raw_b[...] = jnp.zeros_like(raw_b)

    gate = gate_ref[...]
    gain = g_ref[...] * (1.0 + scale_ref[...])
    shift = shift_ref[...]

    def step(cur, prev):
        cur[...] = _dot(m_ref[...], w_ref[...])
        for s in range(x_ref.shape[0] // SLAB):
            r = slice(s * SLAB, (s + 1) * SLAB)
            x1 = x_ref[r, :] + gate * prev[r, :]
            x1_ref[r, :] = x1
            h2_ref[r, :] = _modulated_norm_rows(x1, gain, shift).astype(BF16)

    @pl.when(i % 2 == 0)
    def _():
        step(raw_a, raw_b)

    @pl.when(i % 2 == 1)
    def _():
        step(raw_b, raw_a)


def _out_proj(merged, w_out, x2d, mod4, g2, S, tm=512):
    T, D = x2d.shape
    per_b = S // tm
    n = T // tm
    done = lambda i: jnp.maximum(i - 1, 0)
    mod_spec = lambda k: pl.BlockSpec((None, None, 1, D), lambda i: (done(i) // per_b, k, 0, 0))
    return pl.pallas_call(
        _out_kernel,
        out_shape=(jax.ShapeDtypeStruct((T, D), F32), jax.ShapeDtypeStruct((T, D), BF16)),
        grid=(n + 1,),
        in_specs=[pl.BlockSpec((tm, D), lambda i: (jnp.minimum(i, n - 1), 0)),
                  pl.BlockSpec((D, D), lambda i: (0, 0)),
                  pl.BlockSpec((tm, D), lambda i: (done(i), 0)),
                  mod_spec(2),
                  pl.BlockSpec((1, D), lambda i: (0, 0)),
                  mod_spec(4),
                  mod_spec(3)],
        out_specs=(pl.BlockSpec((tm, D), lambda i: (done(i), 0)),
                   pl.BlockSpec((tm, D), lambda i: (done(i), 0))),
        scratch_shapes=[pltpu.VMEM((tm, D), F32), pltpu.VMEM((tm, D), F32)],
        compiler_params=_params(("arbitrary",)),
        name="out_proj",
    )(merged, w_out, x2d, mod4, g2, mod4, mod4)


def _ffn_kernel(h_ref, w1_ref, w2_ref, x1_ref, gate_ref, g_ref, o_ref, acc_a, acc_b, *, n_tiles):
    i = pl.program_id(0)
    f = pl.program_id(1)

    @pl.when((i == 0) & (f == 0))
    def _():
        acc_b[...] = jnp.zeros_like(acc_b)

    def up_down():
        a = jnp.maximum(_dot(h_ref[...], w1_ref[...]), 0.0)
        return _dot((a * a).astype(BF16), w2_ref[...])

    def finish(prev):
        gate = gate_ref[...]
        g = g_ref[...]
        for s in range(x1_ref.shape[0] // SLAB):
            r = slice(s * SLAB, (s + 1) * SLAB)
            x2 = x1_ref[r, :] + gate * prev[r, :]
            o_ref[r, :] = x2 * lax.rsqrt(jnp.mean(x2 * x2, axis=-1, keepdims=True) + EPS) * g

    for parity, (cur, prev) in enumerate(((acc_a, acc_b), (acc_b, acc_a))):
        mine = i % 2 == parity

        @pl.when(mine & (f == 0) & (i < n_tiles))
        def _(cur=cur, prev=prev):
            cur[...] = up_down()
            finish(prev)

        @pl.when(mine & (f > 0) & (i < n_tiles))
        def _(cur=cur):
            cur[...] += up_down()

        @pl.when(mine & (f == 0) & (i == n_tiles))
        def _(prev=prev):
            finish(prev)


def _ffn(h2, w1, w2, x1, mod4, g_final, S, tm=512, tf=1024):
    T, D = x1.shape
    F = w1.shape[1]
    per_b = S // tm
    n, nf = T // tm, F // tf
    done = lambda i: jnp.maximum(i - 1, 0)
    chunk = lambda i, f: jnp.where(i == n, nf - 1, f)
    return pl.pallas_call(
        functools.partial(_ffn_kernel, n_tiles=n),
        out_shape=jax.ShapeDtypeStruct((T, D), F32),
        grid=(n + 1, nf),
        in_specs=[pl.BlockSpec((tm, D), lambda i, f: (jnp.minimum(i, n - 1), 0)),
                  pl.BlockSpec((D, tf), lambda i, f: (0, chunk(i, f))),
                  pl.BlockSpec((tf, D), lambda i, f: (chunk(i, f), 0)),
                  pl.BlockSpec((tm, D), lambda i, f: (done(i), 0)),
                  pl.BlockSpec((None, None, 1, D), lambda i, f: (done(i) // per_b, 5, 0, 0)),
                  pl.BlockSpec((1, D), lambda i, f: (0, 0))],
        out_specs=pl.BlockSpec((tm, D), lambda i, f: (done(i), 0)),
        scratch_shapes=[pltpu.VMEM((tm, D), F32), pltpu.VMEM((tm, D), F32)],
        compiler_params=_params(("arbitrary", "arbitrary")),
        name="ffn",
    )(h2, w1, w2, x1, mod4, g_final)


def _pad_to(a, rows, cols):
    return jnp.pad(a, ((0, rows - a.shape[0]), (0, cols - a.shape[1])))


def kernel(x, c, w_ada, b_ada, g_norm1, w_in, w_gla_a_up, b_gla_a, g_gla_out, b_fox_f, w_proj_gla,
           w_proj_fox, w_out, g_norm2, w_ff1, w_ff2, g_final):
    B, S, D = x.shape
    assert w_ada.shape[0] == 1, "the final norm is fused into the last layer's MLP kernel"
    l = 0
    x2d = x.reshape(B * S, D)
    mod4 = _ada(c, w_ada[l], b_ada[l][None, :]).reshape(B, N_MOD, 1, D)

    proj, low = _in_proj(x2d, g_norm1[l][None, :], mod4, jnp.swapaxes(w_in[l], 0, 1), S)

    wup_pad = _pad_to(w_gla_a_up[l], LANES, D // 2).astype(BF16)
    o_a = _gla(proj, low, wup_pad, b_gla_a[l][None, :], g_gla_out[l][None, :], B, S, D)

    bf_pad = _pad_to(b_fox_f[l][None, :], 1, LANES)
    cum_col, cum_row = _fox_cum(low, bf_pad, B, S)
    o_b = _fox(proj, cum_col, cum_row, B, S, D)

    merged = _merge(o_a, o_b, w_proj_gla[l].astype(BF16), w_proj_fox[l].astype(BF16), proj, D)
    x1, h2 = _out_proj(merged, w_out[l].astype(BF16), x2d, mod4, g_norm2[l][None, :], S)
    out = _ffn(h2, w_ff1[l].astype(BF16), w_ff2[l].astype(BF16), x1, mod4, g_final[None, :], S)
    return out.reshape(B, S, D)
```

```python
import functools
import math

import jax
import jax.numpy as jnp
from jax import lax
from jax.experimental import pallas as pl
from jax.experimental.pallas import tpu as pltpu

F32 = jnp.float32
BF16 = jnp.bfloat16

EPS = 1e-6
LANES = 128
BF16_ROWS = 16
GLA_HEADS = 4
GLA_RANK = 16
GLA_TAU = 16.0
GLA_CHUNK = 64
FOX_HEADS = 16
N_MOD = 6
NEG_BIG = -1e30
LOG2E = math.log2(math.e)
SLAB = 16
SLAB_UNROLL = 4
VMEM_LIMIT = 60 * 1024 * 1024

_NT = (((1,), (1,)), ((), ()))
_TN = (((0,), (0,)), ((), ()))


def _dot(a, b):
    return jnp.dot(a, b, preferred_element_type=F32)


def _log_sigmoid(z):
    return jnp.minimum(z, 0.0) - jnp.log(1.0 + jnp.exp(-jnp.abs(z)))


def _silu(z):
    return z * jax.nn.sigmoid(z)


def _split_bf16(a):
    hi = a.astype(BF16)
    lo = (a - hi.astype(F32)).astype(BF16)
    return hi, lo


def _params(sem):
    return pltpu.CompilerParams(dimension_semantics=sem, vmem_limit_bytes=VMEM_LIMIT)


def _ada_kernel(c_ref, w_ref, b_ref, o_ref):
    cond = _silu(c_ref[...]).astype(BF16)
    o_ref[...] = _dot(cond, w_ref[...].astype(BF16)) + b_ref[...]


def _ada(c, w_ada, b_ada, tn=1024):
    B, D = c.shape
    N = w_ada.shape[1]
    return pl.pallas_call(
        _ada_kernel,
        out_shape=jax.ShapeDtypeStruct((B, N), F32),
        grid=(N // tn,),
        in_specs=[pl.BlockSpec((B, D), lambda j: (0, 0)),
                  pl.BlockSpec((D, tn), lambda j: (0, j)),
                  pl.BlockSpec((1, tn), lambda j: (0, j))],
        out_specs=pl.BlockSpec((B, tn), lambda j: (0, j)),
        compiler_params=_params(("arbitrary",)),
        name="ada",
    )(c, w_ada, b_ada)


def _modulated_norm_rows(x, gain, shift):
    return x * lax.rsqrt(jnp.mean(x * x, axis=-1, keepdims=True) + EPS) * gain + shift


def _for_row_slabs(n_rows, body):
    def step(s, carry):
        body(pl.ds(pl.multiple_of(s * SLAB, SLAB), SLAB))
        return carry

    lax.fori_loop(0, n_rows // SLAB, step, 0, unroll=SLAB_UNROLL)


def _in_kernel(x_ref, g_ref, scale_ref, shift_ref, wt_ref, wa_ref, wb_ref, proj_ref, low_ref, h_ref):
    @pl.when(pl.program_id(1) == 0)
    def _():
        gain = g_ref[...] * (1.0 + scale_ref[...])
        shift = shift_ref[...]

        def body(r):
            h_ref[r, :] = _modulated_norm_rows(x_ref[r, :], gain, shift).astype(BF16)

        _for_row_slabs(x_ref.shape[0], body)
        w_narrow = jnp.concatenate([wa_ref[...], wb_ref[...]], axis=0).astype(BF16)
        seg = lax.dot_general(h_ref[...], w_narrow, _NT, preferred_element_type=F32)
        lane = lax.broadcasted_iota(jnp.int32, seg.shape, 1)
        keep = (lane < GLA_RANK) | ((lane >= LANES) & (lane < LANES + FOX_HEADS))
        low_ref[...] = jnp.where(keep, seg, 0.0)

    w_t = wt_ref[...].astype(BF16)
    proj_ref[...] = lax.dot_general(h_ref[...], w_t, _NT, preferred_element_type=F32).astype(proj_ref.dtype)


def _in_proj(x2d, g1, mod4, w_t, S, tm=1024, tn=1024):
    T, D = x2d.shape
    N = 8 * D
    per_b = S // tm
    align = math.gcd(tn, GLA_RANK, FOX_HEADS)

    def w_rows(i, j):
        skip = jnp.where(j >= 3 * D // tn, GLA_RANK, 0) + jnp.where(j >= 6 * D // tn, FOX_HEADS, 0)
        return (pl.multiple_of(j * tn + skip, align), 0)

    narrow_spec = lambda row0: pl.BlockSpec((pl.Element(LANES), pl.Element(D)), lambda i, j: (row0, 0))
    return pl.pallas_call(
        _in_kernel,
        out_shape=(jax.ShapeDtypeStruct((T, N), BF16), jax.ShapeDtypeStruct((T, 2 * LANES), F32)),
        grid=(T // tm, N // tn),
        in_specs=[pl.BlockSpec((tm, D), lambda i, j: (i, 0)),
                  pl.BlockSpec((1, D), lambda i, j: (0, 0)),
                  pl.BlockSpec((None, None, 1, D), lambda i, j: (i // per_b, 1, 0, 0)),
                  pl.BlockSpec((None, None, 1, D), lambda i, j: (i // per_b, 0, 0, 0)),
                  pl.BlockSpec((pl.Element(tn), pl.Element(D)), w_rows),
                  narrow_spec(3 * D),
                  narrow_spec(6 * D + GLA_RANK)],
        out_specs=(pl.BlockSpec((tm, tn), lambda i, j: (i, j)),
                   pl.BlockSpec((tm, 2 * LANES), lambda i, j: (i, 0))),
        scratch_shapes=[pltpu.VMEM((tm, D), BF16)],
        compiler_params=_params(("parallel", "arbitrary")),
        name="in_proj",
    )(x2d, g1, mod4, mod4, w_t, w_t, w_t)


def _gla_kernel(q_ref, k_ref, v_ref, gg_ref, low_ref, wup_ref, ba_ref, g_ref, o_ref, *, q_scale, chunk):
    C, SB = chunk, GLA_CHUNK
    S, dk = q_ref.shape
    dv = v_ref.shape[1]
    tril = (lax.broadcasted_iota(jnp.int32, (C, C), 1) <= lax.broadcasted_iota(jnp.int32, (C, C), 0)).astype(BF16)
    wup = wup_ref[...]
    ba = ba_ref[...]
    g = g_ref[...]
    state_t = jnp.zeros((dv, dk), F32)
    for c in range(S // C):
        rows = slice(c * C, (c + 1) * C)
        q = q_ref[rows, :].astype(F32) * q_scale
        k = k_ref[rows, :].astype(F32)
        v = v_ref[rows, :]
        z = _dot(low_ref[rows, :].astype(BF16), wup) + ba
        la = _log_sigmoid(z) * (1.0 / GLA_TAU)
        la_hi, la_lo = _split_bf16(la)
        b = _dot(tril, la_hi) + _dot(tril, la_lo)
        b_last = b[C - 1:C, :]
        q_dec = (q * jnp.exp(b)).astype(BF16)
        k_dec = (k * jnp.exp(b_last - b)).astype(BF16)

        attn_rows = []
        for i in range(C // SB):
            sub = slice(i * SB, (i + 1) * SB)
            n_keys = -(-(i + 1) * SB // LANES) * LANES
            ref = b[i * SB - 1:i * SB, :] if i > 0 else jnp.zeros((1, dk), F32)
            q_loc = (q[sub] * jnp.exp(b[sub] - ref)).astype(BF16)
            k_loc = (k[:n_keys] * jnp.exp(ref - b[:n_keys])).astype(BF16)
            a = lax.dot_general(q_loc, k_loc, _NT, preferred_element_type=F32)
            key_i = lax.broadcasted_iota(jnp.int32, (SB, n_keys), 1)
            qry_i = lax.broadcasted_iota(jnp.int32, (SB, n_keys), 0) + i * SB
            a = jnp.where(key_i <= qry_i, a, 0.0)
            if n_keys < C:
                a = jnp.concatenate([a, jnp.zeros((SB, C - n_keys), F32)], axis=1)
            attn_rows.append(a)
        attn = jnp.concatenate(attn_rows, axis=0).astype(BF16)

        o = _dot(attn, v) + lax.dot_general(q_dec, state_t.astype(BF16), _NT, preferred_element_type=F32)
        state_t = state_t * jnp.exp(b_last) + lax.dot_general(v, k_dec, _TN, preferred_element_type=F32)
        y = o * lax.rsqrt(jnp.mean(o * o, axis=-1, keepdims=True) + EPS) * g
        o_ref[rows, :] = (y * _silu(gg_ref[rows, :].astype(F32))).astype(o_ref.dtype)


def _gla(proj, low, wup_pad, b_a, g_out, B, S, D, chunk=256):
    T = B * S
    H = GLA_HEADS
    dk = (D // 2) // H
    dv = D // H
    k_blk0 = (D // 2) // dk
    v_blk0 = D // dv
    gg_blk0 = 2 * D // dv
    return pl.pallas_call(
        functools.partial(_gla_kernel, q_scale=dk ** -0.5, chunk=chunk),
        out_shape=jax.ShapeDtypeStruct((T, D), BF16),
        grid=(B, H),
        in_specs=[pl.BlockSpec((S, dk), lambda b, h: (b, h)),
                  pl.BlockSpec((S, dk), lambda b, h: (b, k_blk0 + h)),
                  pl.BlockSpec((S, dv), lambda b, h: (b, v_blk0 + h)),
                  pl.BlockSpec((S, dv), lambda b, h: (b, gg_blk0 + h)),
                  pl.BlockSpec((S, LANES), lambda b, h: (b, 0)),
                  pl.BlockSpec((LANES, dk), lambda b, h: (0, h)),
                  pl.BlockSpec((1, dk), lambda b, h: (0, h)),
                  pl.BlockSpec((1, dv), lambda b, h: (0, 0))],
        out_specs=pl.BlockSpec((S, dv), lambda b, h: (b, h)),
        compiler_params=_params(("parallel", "arbitrary")),
        name="gla",
    )(proj, proj, proj, proj, low, wup_pad, b_a, g_out)


def _cum_kernel(low_ref, bf_ref, col_ref, row_ref, *, blk):
    S = low_ref.shape[0]
    n_heads = row_ref.shape[0]
    r_i = lax.broadcasted_iota(jnp.int32, (blk, blk), 0)
    c_i = lax.broadcasted_iota(jnp.int32, (blk, blk), 1)
    tril = (c_i <= r_i).astype(BF16)
    triu = (r_i <= c_i).astype(BF16)
    bf = bf_ref[...]
    carry_row = jnp.zeros((1, LANES), F32)
    carry_col = jnp.zeros((LANES, 1), F32)
    for i in range(S // blk):
        r = slice(i * blk, (i + 1) * blk)
        lf = _log_sigmoid(low_ref[r, :] + bf)
        hi, lo = _split_bf16(lf)
        loc = _dot(tril, hi) + _dot(tril, lo) + carry_row
        col_ref[r, :] = loc
        carry_row = loc[blk - 1:blk, :]
        loc_t = (lax.dot_general(hi, triu, _TN, preferred_element_type=F32)
                 + lax.dot_general(lo, triu, _TN, preferred_element_type=F32) + carry_col)
        row_ref[:, r] = loc_t[:n_heads, :]
        carry_col = loc_t[:, blk - 1:blk]


def _fox_cum(low, bf_pad, B, S, blk=128):
    return pl.pallas_call(
        functools.partial(_cum_kernel, blk=blk),
        out_shape=(jax.ShapeDtypeStruct((B, S, LANES), F32),
                   jax.ShapeDtypeStruct((B, FOX_HEADS, S), F32)),
        grid=(B,),
        in_specs=[pl.BlockSpec((S, LANES), lambda b: (b, 1)),
                  pl.BlockSpec((1, LANES), lambda b: (0, 0))],
        out_specs=(pl.BlockSpec((None, S, LANES), lambda b: (b, 0, 0)),
                   pl.BlockSpec((None, FOX_HEADS, S), lambda b: (b, 0, 0))),
        compiler_params=_params(("arbitrary",)),
        name="fox_cum",
    )(low, bf_pad)


def _fox_kernel(q_ref, k_ref, v_ref, cc_ref, cr_ref, o_ref, qs_ref, m_ref, l_ref, acc_ref, *, blk, hd, scale):
    S = q_ref.shape[0]
    n_heads = q_ref.shape[1] // hd
    lane = lax.broadcasted_iota(jnp.int32, (S, LANES), 1)
    key_i = lax.broadcasted_iota(jnp.int32, (blk, blk), 0)
    qry_i = lax.broadcasted_iota(jnp.int32, (blk, blk), 1)
    causal = key_i <= qry_i
    ones_rows = (lax.broadcasted_iota(jnp.int32, (BF16_ROWS, blk), 0) == 0).astype(BF16)

    for g in range(n_heads):
        h = pl.program_id(1) * n_heads + g
        cols = slice(g * hd, (g + 1) * hd)
        qs_ref[g] = (q_ref[:, cols].astype(F32) * (scale * LOG2E)).astype(BF16)
        ck_all = jnp.sum(jnp.where(lane == h, cc_ref[...], 0.0), axis=-1, keepdims=True) * LOG2E
        cq_all = cr_ref[pl.ds(h, 1), :] * LOG2E
        for j in range(S // blk):
            r0 = j * blk
            ks = slice(r0, r0 + blk)
            t = lax.dot_general(k_ref[ks, cols], qs_ref[g, r0:, :], _NT, preferred_element_type=F32) - ck_all[ks]
            t_diag = jnp.where(causal, t[:, :blk], NEG_BIG)
            t = t_diag if r0 + blk == S else jnp.concatenate([t_diag, t[:, blk:]], axis=1)
            cq = cq_all[:, r0:]
            t_max = jnp.max(t, axis=0, keepdims=True) + cq
            if j == 0:
                m_new = t_max
            else:
                m_old = m_ref[g, :, r0:]
                m_new = jnp.maximum(m_old, t_max)
                alpha = jnp.exp2(m_old - m_new)
            p = jnp.exp2(t - (m_new - cq))
            pv_l = _dot(jnp.concatenate([v_ref[ks, cols].T, ones_rows], axis=0), p.astype(BF16))
            pv, p_sum = pv_l[:hd], pv_l[hd:hd + 1]
            if j == 0:
                l_ref[g] = p_sum
                acc_ref[g] = pv
            else:
                l_ref[g, :, r0:] = alpha * l_ref[g, :, r0:] + p_sum
                acc_ref[g, :, r0:] = alpha * acc_ref[g, :, r0:] + pv
            m_ref[g, :, r0:] = m_new
        o_ref[:, cols] = (acc_ref[g] / l_ref[g]).T.astype(o_ref.dtype)


def _fox(proj, cum_col, cum_row, B, S, D, blk=256, heads_per_step=4):
    T = B * S
    H = FOX_HEADS
    hd = D // H
    G = heads_per_step
    w = G * hd
    q_blk0, k_blk0, v_blk0 = 3 * D // w, 4 * D // w, 5 * D // w
    return pl.pallas_call(
        functools.partial(_fox_kernel, blk=blk, hd=hd, scale=1.0 / math.sqrt(hd)),
        out_shape=jax.ShapeDtypeStruct((T, D), BF16),
        grid=(B, H // G),
        in_specs=[pl.BlockSpec((S, w), lambda b, h: (b, q_blk0 + h)),
                  pl.BlockSpec((S, w), lambda b, h: (b, k_blk0 + h)),
                  pl.BlockSpec((S, w), lambda b, h: (b, v_blk0 + h)),
                  pl.BlockSpec((None, S, LANES), lambda b, h: (b, 0, 0)),
                  pl.BlockSpec((None, H, S), lambda b, h: (b, 0, 0))],
        out_specs=pl.BlockSpec((S, w), lambda b, h: (b, h)),
        scratch_shapes=[pltpu.VMEM((G, S, hd), BF16), pltpu.VMEM((G, 1, S), F32), pltpu.VMEM((G, 1, S), F32),
                        pltpu.VMEM((G, hd, S), F32)],
        compiler_params=_params(("parallel", "arbitrary")),
        name="fox",
    )(proj, proj, proj, cum_col, cum_row)


def _merge_kernel(oa_ref, ob_ref, wa_ref, wb_ref, ga_ref, gb_ref, o_ref):
    a = _dot(oa_ref[...], wa_ref[...])
    b = _dot(ob_ref[...], wb_ref[...])
    merged = jax.nn.sigmoid(ga_ref[...].astype(F32)) * a + jax.nn.sigmoid(gb_ref[...].astype(F32)) * b
    o_ref[...] = merged.astype(o_ref.dtype)


def _merge(o_a, o_b, w_a, w_b, proj, D, tm=512, tn=1024):
    T = o_a.shape[0]
    ga_blk0, gb_blk0 = 6 * D // tn, 7 * D // tn
    return pl.pallas_call(
        _merge_kernel,
        out_shape=jax.ShapeDtypeStruct((T, D), BF16),
        grid=(D // tn, T // tm),
        in_specs=[pl.BlockSpec((tm, D), lambda j, i: (i, 0)),
                  pl.BlockSpec((tm, D), lambda j, i: (i, 0)),
                  pl.BlockSpec((D, tn), lambda j, i: (0, j)),
                  pl.BlockSpec((D, tn), lambda j, i: (0, j)),
                  pl.BlockSpec((tm, tn), lambda j, i: (i, ga_blk0 + j)),
                  pl.BlockSpec((tm, tn), lambda j, i: (i, gb_blk0 + j))],
        out_specs=pl.BlockSpec((tm, tn), lambda j, i: (i, j)),
        compiler_params=_params(("arbitrary", "arbitrary")),
        name="merge",
    )(o_a, o_b, w_a, w_b, proj, proj)


def _out_kernel(m_ref, w_ref, x_ref, gate_ref, g_ref, scale_ref, shift_ref, x1_ref, h2_ref, raw_a, raw_b):
    i = pl.program_id(0)

    @pl.when(i == 0)
    def _():
        raw_b[...] = jnp.zeros_like(raw_b)

    gate = gate_ref[...]
    gain = g_ref[...] * (1.0 + scale_ref[...])
    shift = shift_ref[...]

    def step(cur, prev):
        cur[...] = _dot(m_ref[...], w_ref[...])
        for s in range(x_ref.shape[0] // SLAB):
            r = slice(s * SLAB, (s + 1) * SLAB)
            x1 = x_ref[r, :] + gate * prev[r, :]
            x1_ref[r, :] = x1
            h2_ref[r, :] = _modulated_norm_rows(x1, gain, shift).astype(BF16)

    @pl.when(i % 2 == 0)
    def _():
        step(raw_a, raw_b)

    @pl.when(i % 2 == 1)
    def _():
        step(raw_b, raw_a)


def _out_proj(merged, w_out, x2d, mod4, g2, S, tm=512):
    T, D = x2d.shape
    per_b = S // tm
    n = T // tm
    done = lambda i: jnp.maximum(i - 1, 0)
    mod_spec = lambda k: pl.BlockSpec((None, None, 1, D), lambda i: (done(i) // per_b, k, 0, 0))
    return pl.pallas_call(
        _out_kernel,
        out_shape=(jax.ShapeDtypeStruct((T, D), F32), jax.ShapeDtypeStruct((T, D), BF16)),
        grid=(n + 1,),
        in_specs=[pl.BlockSpec((tm, D), lambda i: (jnp.minimum(i, n - 1), 0)),
                  pl.BlockSpec((D, D), lambda i: (0, 0)),
                  pl.BlockSpec((tm, D), lambda i: (done(i), 0)),
                  mod_spec(2),
                  pl.BlockSpec((1, D), lambda i: (0, 0)),
                  mod_spec(4),
                  mod_spec(3)],
        out_specs=(pl.BlockSpec((tm, D), lambda i: (done(i), 0)),
                   pl.BlockSpec((tm, D), lambda i: (done(i), 0))),
        scratch_shapes=[pltpu.VMEM((tm, D), F32), pltpu.VMEM((tm, D), F32)],
        compiler_params=_params(("arbitrary",)),
        name="out_proj",
    )(merged, w_out, x2d, mod4, g2, mod4, mod4)


def _ffn_kernel(h_ref, w1_ref, w2_ref, x1_ref, gate_ref, g_ref, o_ref):
    f = pl.program_id(1)

    def up_down():
        a = jnp.maximum(_dot(h_ref[...], w1_ref[...]), 0.0)
        return _dot((a * a).astype(BF16), w2_ref[...])

    @pl.when(f == 0)
    def _():
        o_ref[...] = up_down()

    @pl.when(f > 0)
    def _():
        o_ref[...] += up_down()

    @pl.when(f == pl.num_programs(1) - 1)
    def _():
        gate = gate_ref[...]
        g = g_ref[...]

        for s in range(x1_ref.shape[0] // SLAB):
            r = slice(s * SLAB, (s + 1) * SLAB)
            x2 = x1_ref[r, :] + gate * o_ref[r, :]
            o_ref[r, :] = x2 * lax.rsqrt(jnp.mean(x2 * x2, axis=-1, keepdims=True) + EPS) * g


def _ffn(h2, w1, w2, x1, mod4, g_final, S, tm=1024, tf=512):
    T, D = x1.shape
    F = w1.shape[1]
    per_b = S // tm
    return pl.pallas_call(
        _ffn_kernel,
        out_shape=jax.ShapeDtypeStruct((T, D), F32),
        grid=(T // tm, F // tf),
        in_specs=[pl.BlockSpec((tm, D), lambda i, f: (i, 0)),
                  pl.BlockSpec((D, tf), lambda i, f: (0, f)),
                  pl.BlockSpec((tf, D), lambda i, f: (f, 0)),
                  pl.BlockSpec((tm, D), lambda i, f: (i, 0)),
                  pl.BlockSpec((None, None, 1, D), lambda i, f: (i // per_b, 5, 0, 0)),
                  pl.BlockSpec((1, D), lambda i, f: (0, 0))],
        out_specs=pl.BlockSpec((tm, D), lambda i, f: (i, 0)),
        compiler_params=_params(("parallel", "arbitrary")),
        name="ffn",
    )(h2, w1, w2, x1, mod4, g_final)


def _pad_to(a, rows, cols):
    return jnp.pad(a, ((0, rows - a.shape[0]), (0, cols - a.shape[1])))


def kernel(x, c, w_ada, b_ada, g_norm1, w_in, w_gla_a_up, b_gla_a, g_gla_out, b_fox_f, w_proj_gla,
           w_proj_fox, w_out, g_norm2, w_ff1, w_ff2, g_final):
    B, S, D = x.shape
    assert w_ada.shape[0] == 1, "the final norm is fused into the last layer's MLP kernel"
    l = 0
    x2d = x.reshape(B * S, D)
    mod4 = _ada(c, w_ada[l], b_ada[l][None, :]).reshape(B, N_MOD, 1, D)

    proj, low = _in_proj(x2d, g_norm1[l][None, :], mod4, jnp.swapaxes(w_in[l], 0, 1), S)

    wup_pad = _pad_to(w_gla_a_up[l], LANES, D // 2).astype(BF16)
    o_a = _gla(proj, low, wup_pad, b_gla_a[l][None, :], g_gla_out[l][None, :], B, S, D)

    bf_pad = _pad_to(b_fox_f[l][None, :], 1, LANES)
    cum_col, cum_row = _fox_cum(low, bf_pad, B, S)
    o_b = _fox(proj, cum_col, cum_row, B, S, D)

    merged = _merge(o_a, o_b, w_proj_gla[l].astype(BF16), w_proj_fox[l].astype(BF16), proj, D)
    x1, h2 = _out_proj(merged, w_out[l].astype(BF16), x2d, mod4, g_norm2[l][None, :], S)
    out = _ffn(h2, w_ff1[l].astype(BF16), w_ff2[l].astype(BF16), x1, mod4, g_final[None, :], S)
    return out.reshape(B, S, D)
```

```python
import functools
import math

import jax
import jax.numpy as jnp
from jax import lax
from jax.experimental import pallas as pl
from jax.experimental.pallas import tpu as pltpu

F32 = jnp.float32
BF16 = jnp.bfloat16

EPS = 1e-6
LANES = 128
BF16_ROWS = 16
GLA_HEADS = 4
GLA_RANK = 16
GLA_TAU = 16.0
GLA_CHUNK = 64
FOX_HEADS = 16
N_MOD = 6
NEG_BIG = -1e30
LOG2E = math.log2(math.e)
SLAB = 16
SLAB_UNROLL = 4
VMEM_LIMIT = 56 * 1024 * 1024

_NT = (((1,), (1,)), ((), ()))
_TN = (((0,), (0,)), ((), ()))


def _dot(a, b):
    return jnp.dot(a, b, preferred_element_type=F32)


def _log_sigmoid(z):
    return jnp.minimum(z, 0.0) - jnp.log(1.0 + jnp.exp(-jnp.abs(z)))


def _silu(z):
    return z * jax.nn.sigmoid(z)


def _split_bf16(a):
    hi = a.astype(BF16)
    lo = (a - hi.astype(F32)).astype(BF16)
    return hi, lo


def _params(sem):
    return pltpu.CompilerParams(dimension_semantics=sem, vmem_limit_bytes=VMEM_LIMIT)


def _ada_kernel(c_ref, w_ref, b_ref, o_ref):
    cond = _silu(c_ref[...]).astype(BF16)
    o_ref[...] = _dot(cond, w_ref[...].astype(BF16)) + b_ref[...]


def _ada(c, w_ada, b_ada, tn=1024):
    B, D = c.shape
    N = w_ada.shape[1]
    return pl.pallas_call(
        _ada_kernel,
        out_shape=jax.ShapeDtypeStruct((B, N), F32),
        grid=(N // tn,),
        in_specs=[pl.BlockSpec((B, D), lambda j: (0, 0)),
                  pl.BlockSpec((D, tn), lambda j: (0, j)),
                  pl.BlockSpec((1, tn), lambda j: (0, j))],
        out_specs=pl.BlockSpec((B, tn), lambda j: (0, j)),
        compiler_params=_params(("arbitrary",)),
        name="ada",
    )(c, w_ada, b_ada)


def _modulated_norm_rows(x, gain, shift):
    return x * lax.rsqrt(jnp.mean(x * x, axis=-1, keepdims=True) + EPS) * gain + shift


def _for_row_slabs(n_rows, body):
    def step(s, carry):
        body(pl.ds(pl.multiple_of(s * SLAB, SLAB), SLAB))
        return carry

    lax.fori_loop(0, n_rows // SLAB, step, 0, unroll=SLAB_UNROLL)


def _in_kernel(x_ref, g_ref, scale_ref, shift_ref, wt_ref, wa_ref, wb_ref, proj_ref, low_ref, h_ref):
    @pl.when(pl.program_id(1) == 0)
    def _():
        gain = g_ref[...] * (1.0 + scale_ref[...])
        shift = shift_ref[...]

        def body(r):
            h_ref[r, :] = _modulated_norm_rows(x_ref[r, :], gain, shift).astype(BF16)

        _for_row_slabs(x_ref.shape[0], body)
        w_narrow = jnp.concatenate([wa_ref[...], wb_ref[...]], axis=0).astype(BF16)
        seg = lax.dot_general(h_ref[...], w_narrow, _NT, preferred_element_type=F32)
        lane = lax.broadcasted_iota(jnp.int32, seg.shape, 1)
        keep = (lane < GLA_RANK) | ((lane >= LANES) & (lane < LANES + FOX_HEADS))
        low_ref[...] = jnp.where(keep, seg, 0.0)

    w_t = wt_ref[...].astype(BF16)
    proj_ref[...] = lax.dot_general(h_ref[...], w_t, _NT, preferred_element_type=F32).astype(proj_ref.dtype)


def _in_proj(x2d, g1, mod4, w_t, S, tm=1024, tn=1024):
    T, D = x2d.shape
    N = 8 * D
    per_b = S // tm
    align = math.gcd(tn, GLA_RANK, FOX_HEADS)

    def w_rows(i, j):
        skip = jnp.where(j >= 3 * D // tn, GLA_RANK, 0) + jnp.where(j >= 6 * D // tn, FOX_HEADS, 0)
        return (pl.multiple_of(j * tn + skip, align), 0)

    narrow_spec = lambda row0: pl.BlockSpec((pl.Element(LANES), pl.Element(D)), lambda i, j: (row0, 0))
    return pl.pallas_call(
        _in_kernel,
        out_shape=(jax.ShapeDtypeStruct((T, N), BF16), jax.ShapeDtypeStruct((T, 2 * LANES), F32)),
        grid=(T // tm, N // tn),
        in_specs=[pl.BlockSpec((tm, D), lambda i, j: (i, 0)),
                  pl.BlockSpec((1, D), lambda i, j: (0, 0)),
                  pl.BlockSpec((None, None, 1, D), lambda i, j: (i // per_b, 1, 0, 0)),
                  pl.BlockSpec((None, None, 1, D), lambda i, j: (i // per_b, 0, 0, 0)),
                  pl.BlockSpec((pl.Element(tn), pl.Element(D)), w_rows),
                  narrow_spec(3 * D),
                  narrow_spec(6 * D + GLA_RANK)],
        out_specs=(pl.BlockSpec((tm, tn), lambda i, j: (i, j)),
                   pl.BlockSpec((tm, 2 * LANES), lambda i, j: (i, 0))),
        scratch_shapes=[pltpu.VMEM((tm, D), BF16)],
        compiler_params=_params(("parallel", "arbitrary")),
        name="in_proj",
    )(x2d, g1, mod4, mod4, w_t, w_t, w_t)


def _gla_kernel(q_ref, k_ref, v_ref, gg_ref, low_ref, wup_ref, ba_ref, g_ref, o_ref, *, q_scale, chunk, dk, dv):
    C, SB = chunk, GLA_CHUNK
    S = q_ref.shape[0]
    n_heads = q_ref.shape[1] // dk
    tril = (lax.broadcasted_iota(jnp.int32, (C, C), 1) <= lax.broadcasted_iota(jnp.int32, (C, C), 0)).astype(BF16)
    g = g_ref[...]
    states = [jnp.zeros((dv, dk), F32) for _ in range(n_heads)]
    for c, h in ((c, h) for c in range(S // C) for h in range(n_heads)):
        rows = slice(c * C, (c + 1) * C)
        kc, vc = slice(h * dk, (h + 1) * dk), slice(h * dv, (h + 1) * dv)
        state_t = states[h]
        q = q_ref[rows, kc].astype(F32) * q_scale
        k = k_ref[rows, kc].astype(F32)
        v = v_ref[rows, vc]
        z = _dot(low_ref[rows, :].astype(BF16), wup_ref[:, kc]) + ba_ref[:, kc]
        la = _log_sigmoid(z) * (1.0 / GLA_TAU)
        la_hi, la_lo = _split_bf16(la)
        b = _dot(tril, la_hi) + _dot(tril, la_lo)
        b_last = b[C - 1:C, :]
        q_dec = (q * jnp.exp(b)).astype(BF16)
        k_dec = (k * jnp.exp(b_last - b)).astype(BF16)

        attn_rows = []
        for i in range(C // SB):
            sub = slice(i * SB, (i + 1) * SB)
            n_keys = -(-(i + 1) * SB // LANES) * LANES
            ref = b[i * SB - 1:i * SB, :] if i > 0 else jnp.zeros((1, dk), F32)
            q_loc = (q[sub] * jnp.exp(b[sub] - ref)).astype(BF16)
            k_loc = (k[:n_keys] * jnp.exp(ref - b[:n_keys])).astype(BF16)
            a = lax.dot_general(q_loc, k_loc, _NT, preferred_element_type=F32)
            key_i = lax.broadcasted_iota(jnp.int32, (SB, n_keys), 1)
            qry_i = lax.broadcasted_iota(jnp.int32, (SB, n_keys), 0) + i * SB
            a = jnp.where(key_i <= qry_i, a, 0.0)
            if n_keys < C:
                a = jnp.concatenate([a, jnp.zeros((SB, C - n_keys), F32)], axis=1)
            attn_rows.append(a)
        attn = jnp.concatenate(attn_rows, axis=0).astype(BF16)

        o = _dot(attn, v) + lax.dot_general(q_dec, state_t.astype(BF16), _NT, preferred_element_type=F32)
        states[h] = state_t * jnp.exp(b_last) + lax.dot_general(v, k_dec, _TN, preferred_element_type=F32)
        y = o * lax.rsqrt(jnp.mean(o * o, axis=-1, keepdims=True) + EPS) * g
        o_ref[rows, vc] = (y * _silu(gg_ref[rows, vc].astype(F32))).astype(o_ref.dtype)


def _gla(proj, low, wup_pad, b_a, g_out, B, S, D, chunk=256, heads_per_step=2):
    T = B * S
    H = GLA_HEADS
    G = heads_per_step
    dk = (D // 2) // H
    dv = D // H
    wk, wv = G * dk, G * dv
    k_blk0 = (D // 2) // wk
    v_blk0 = D // wv
    gg_blk0 = 2 * D // wv
    return pl.pallas_call(
        functools.partial(_gla_kernel, q_scale=dk ** -0.5, chunk=chunk, dk=dk, dv=dv),
        out_shape=jax.ShapeDtypeStruct((T, D), BF16),
        grid=(B, H // G),
        in_specs=[pl.BlockSpec((S, wk), lambda b, h: (b, h)),
                  pl.BlockSpec((S, wk), lambda b, h: (b, k_blk0 + h)),
                  pl.BlockSpec((S, wv), lambda b, h: (b, v_blk0 + h)),
                  pl.BlockSpec((S, wv), lambda b, h: (b, gg_blk0 + h)),
                  pl.BlockSpec((S, LANES), lambda b, h: (b, 0)),
                  pl.BlockSpec((LANES, wk), lambda b, h: (0, h)),
                  pl.BlockSpec((1, wk), lambda b, h: (0, h)),
                  pl.BlockSpec((1, dv), lambda b, h: (0, 0))],
        out_specs=pl.BlockSpec((S, wv), lambda b, h: (b, h)),
        compiler_params=_params(("parallel", "arbitrary")),
        name="gla",
    )(proj, proj, proj, proj, low, wup_pad, b_a, g_out)


def _cum_kernel(low_ref, bf_ref, col_ref, row_ref, *, blk):
    S = low_ref.shape[0]
    n_heads = row_ref.shape[0]
    r_i = lax.broadcasted_iota(jnp.int32, (blk, blk), 0)
    c_i = lax.broadcasted_iota(jnp.int32, (blk, blk), 1)
    tril = (c_i <= r_i).astype(BF16)
    triu = (r_i <= c_i).astype(BF16)
    bf = bf_ref[...]
    carry_row = jnp.zeros((1, LANES), F32)
    carry_col = jnp.zeros((LANES, 1), F32)
    for i in range(S // blk):
        r = slice(i * blk, (i + 1) * blk)
        lf = _log_sigmoid(low_ref[r, :] + bf)
        hi, lo = _split_bf16(lf)
        loc = _dot(tril, hi) + _dot(tril, lo) + carry_row
        col_ref[r, :] = loc
        carry_row = loc[blk - 1:blk, :]
        loc_t = (lax.dot_general(hi, triu, _TN, preferred_element_type=F32)
                 + lax.dot_general(lo, triu, _TN, preferred_element_type=F32) + carry_col)
        row_ref[:, r] = loc_t[:n_heads, :]
        carry_col = loc_t[:, blk - 1:blk]


def _fox_cum(low, bf_pad, B, S, blk=128):
    return pl.pallas_call(
        functools.partial(_cum_kernel, blk=blk),
        out_shape=(jax.ShapeDtypeStruct((B, S, LANES), F32),
                   jax.ShapeDtypeStruct((B, FOX_HEADS, S), F32)),
        grid=(B,),
        in_specs=[pl.BlockSpec((S, LANES), lambda b: (b, 1)),
                  pl.BlockSpec((1, LANES), lambda b: (0, 0))],
        out_specs=(pl.BlockSpec((None, S, LANES), lambda b: (b, 0, 0)),
                   pl.BlockSpec((None, FOX_HEADS, S), lambda b: (b, 0, 0))),
        compiler_params=_params(("arbitrary",)),
        name="fox_cum",
    )(low, bf_pad)


def _fox_kernel(q_ref, k_ref, v_ref, cc_ref, cr_ref, o_ref, qs_ref, m_ref, l_ref, acc_ref, *, blk, hd, scale):
    S = q_ref.shape[0]
    n_heads = q_ref.shape[1] // hd
    lane = lax.broadcasted_iota(jnp.int32, (S, LANES), 1)
    key_i = lax.broadcasted_iota(jnp.int32, (blk, blk), 0)
    qry_i = lax.broadcasted_iota(jnp.int32, (blk, blk), 1)
    causal = key_i <= qry_i
    ones_rows = (lax.broadcasted_iota(jnp.int32, (BF16_ROWS, blk), 0) == 0).astype(BF16)

    for g in range(n_heads):
        h = pl.program_id(1) * n_heads + g
        cols = slice(g * hd, (g + 1) * hd)
        qs_ref[g] = (q_ref[:, cols].astype(F32) * (scale * LOG2E)).astype(BF16)
        ck_all = jnp.sum(jnp.where(lane == h, cc_ref[...], 0.0), axis=-1, keepdims=True) * LOG2E
        cq_all = cr_ref[pl.ds(h, 1), :] * LOG2E
        for j in range(S // blk):
            r0 = j * blk
            ks = slice(r0, r0 + blk)
            t = lax.dot_general(k_ref[ks, cols], qs_ref[g, r0:, :], _NT, preferred_element_type=F32) - ck_all[ks]
            t_diag = jnp.where(causal, t[:, :blk], NEG_BIG)
            t = t_diag if r0 + blk == S else jnp.concatenate([t_diag, t[:, blk:]], axis=1)
            cq = cq_all[:, r0:]
            t_max = jnp.max(t, axis=0, keepdims=True) + cq
            if j == 0:
                m_new = t_max
            else:
                m_old = m_ref[g, :, r0:]
                m_new = jnp.maximum(m_old, t_max)
                alpha = jnp.exp2(m_old - m_new)
            p = jnp.exp2(t - (m_new - cq))
            pv_l = _dot(jnp.concatenate([v_ref[ks, cols].T, ones_rows], axis=0), p.astype(BF16))
            pv, p_sum = pv_l[:hd], pv_l[hd:hd + 1]
            if j == 0:
                l_ref[g] = p_sum
                acc_ref[g] = pv
            else:
                l_ref[g, :, r0:] = alpha * l_ref[g, :, r0:] + p_sum
                acc_ref[g, :, r0:] = alpha * acc_ref[g, :, r0:] + pv
            m_ref[g, :, r0:] = m_new
        o_ref[:, cols] = (acc_ref[g] / l_ref[g]).T.astype(o_ref.dtype)


def _fox(proj, cum_col, cum_row, B, S, D, blk=256, heads_per_step=4):
    T = B * S
    H = FOX_HEADS
    hd = D // H
    G = heads_per_step
    w = G * hd
    q_blk0, k_blk0, v_blk0 = 3 * D // w, 4 * D // w, 5 * D // w
    return pl.pallas_call(
        functools.partial(_fox_kernel, blk=blk, hd=hd, scale=1.0 / math.sqrt(hd)),
        out_shape=jax.ShapeDtypeStruct((T, D), BF16),
        grid=(B, H // G),
        in_specs=[pl.BlockSpec((S, w), lambda b, h: (b, q_blk0 + h)),
                  pl.BlockSpec((S, w), lambda b, h: (b, k_blk0 + h)),
                  pl.BlockSpec((S, w), lambda b, h: (b, v_blk0 + h)),
                  pl.BlockSpec((None, S, LANES), lambda b, h: (b, 0, 0)),
                  pl.BlockSpec((None, H, S), lambda b, h: (b, 0, 0))],
        out_specs=pl.BlockSpec((S, w), lambda b, h: (b, h)),
        scratch_shapes=[pltpu.VMEM((G, S, hd), BF16), pltpu.VMEM((G, 1, S), F32), pltpu.VMEM((G, 1, S), F32),
                        pltpu.VMEM((G, hd, S), F32)],
        compiler_params=_params(("parallel", "arbitrary")),
        name="fox",
    )(proj, proj, proj, cum_col, cum_row)


def _merge_kernel(oa_ref, ob_ref, wa_ref, wb_ref, ga_ref, gb_ref, o_ref):
    a = _dot(oa_ref[...], wa_ref[...])
    b = _dot(ob_ref[...], wb_ref[...])
    merged = jax.nn.sigmoid(ga_ref[...].astype(F32)) * a + jax.nn.sigmoid(gb_ref[...].astype(F32)) * b
    o_ref[...] = merged.astype(o_ref.dtype)


def _merge(o_a, o_b, w_a, w_b, proj, D, tm=512, tn=1024):
    T = o_a.shape[0]
    ga_blk0, gb_blk0 = 6 * D // tn, 7 * D // tn
    return pl.pallas_call(
        _merge_kernel,
        out_shape=jax.ShapeDtypeStruct((T, D), BF16),
        grid=(D // tn, T // tm),
        in_specs=[pl.BlockSpec((tm, D), lambda j, i: (i, 0)),
                  pl.BlockSpec((tm, D), lambda j, i: (i, 0)),
                  pl.BlockSpec((D, tn), lambda j, i: (0, j)),
                  pl.BlockSpec((D, tn), lambda j, i: (0, j)),
                  pl.BlockSpec((tm, tn), lambda j, i: (i, ga_blk0 + j)),
                  pl.BlockSpec((tm, tn), lambda j, i: (i, gb_blk0 + j))],
        out_specs=pl.BlockSpec((tm, tn), lambda j, i: (i, j)),
        compiler_params=_params(("arbitrary", "arbitrary")),
        name="merge",
    )(o_a, o_b, w_a, w_b, proj, proj)


def _out_kernel(m_ref, w_ref, x_ref, gate_ref, g_ref, scale_ref, shift_ref, x1_ref, h2_ref, raw_a, raw_b):
    i = pl.program_id(0)

    @pl.when(i == 0)
    def _():
        raw_b[...] = jnp.zeros_like(raw_b)

    gate = gate_ref[...]
    gain = g_ref[...] * (1.0 + scale_ref[...])
    shift = shift_ref[...]

    def step(cur, prev):
        cur[...] = _dot(m_ref[...], w_ref[...])
        for s in range(x_ref.shape[0] // SLAB):
            r = slice(s * SLAB, (s + 1) * SLAB)
            x1 = x_ref[r, :] + gate * prev[r, :]
            x1_ref[r, :] = x1
            h2_ref[r, :] = _modulated_norm_rows(x1, gain, shift).astype(BF16)

    @pl.when(i % 2 == 0)
    def _():
        step(raw_a, raw_b)

    @pl.when(i % 2 == 1)
    def _():
        step(raw_b, raw_a)


def _out_proj(merged, w_out, x2d, mod4, g2, S, tm=512):
    T, D = x2d.shape
    per_b = S // tm
    n = T // tm
    done = lambda i: jnp.maximum(i - 1, 0)
    mod_spec = lambda k: pl.BlockSpec((None, None, 1, D), lambda i: (done(i) // per_b, k, 0, 0))
    return pl.pallas_call(
        _out_kernel,
        out_shape=(jax.ShapeDtypeStruct((T, D), F32), jax.ShapeDtypeStruct((T, D), BF16)),
        grid=(n + 1,),
        in_specs=[pl.BlockSpec((tm, D), lambda i: (jnp.minimum(i, n - 1), 0)),
                  pl.BlockSpec((D, D), lambda i: (0, 0)),
                  pl.BlockSpec((tm, D), lambda i: (done(i), 0)),
                  mod_spec(2),
                  pl.BlockSpec((1, D), lambda i: (0, 0)),
                  mod_spec(4),
                  mod_spec(3)],
        out_specs=(pl.BlockSpec((tm, D), lambda i: (done(i), 0)),
                   pl.BlockSpec((tm, D), lambda i: (done(i), 0))),
        scratch_shapes=[pltpu.VMEM((tm, D), F32), pltpu.VMEM((tm, D), F32)],
        compiler_params=_params(("arbitrary",)),
        name="out_proj",
    )(merged, w_out, x2d, mod4, g2, mod4, mod4)


def _ffn_kernel(h_ref, w1_ref, w2_ref, x1_ref, gate_ref, g_ref, o_ref, acc_a, acc_b, *, n_tiles):
    i = pl.program_id(0)
    f = pl.program_id(1)

    @pl.when((i == 0) & (f == 0))
    def _():
        acc_b[...] = jnp.zeros_like(acc_b)

    def up_down():
        a = jnp.maximum(_dot(h_ref[...], w1_ref[...]), 0.0)
        return _dot((a * a).astype(BF16), w2_ref[...])

    def finish(prev):
        gate = gate_ref[...]
        g = g_ref[...]
        for s in range(x1_ref.shape[0] // SLAB):
            r = slice(s * SLAB, (s + 1) * SLAB)
            x2 = x1_ref[r, :] + gate * prev[r, :]
            o_ref[r, :] = x2 * lax.rsqrt(jnp.mean(x2 * x2, axis=-1, keepdims=True) + EPS) * g

    for parity, (cur, prev) in enumerate(((acc_a, acc_b), (acc_b, acc_a))):
        mine = i % 2 == parity

        @pl.when(mine & (f == 0) & (i < n_tiles))
        def _(cur=cur, prev=prev):
            cur[...] = up_down()
            finish(prev)

        @pl.when(mine & (f > 0) & (i < n_tiles))
        def _(cur=cur):
            cur[...] += up_down()

        @pl.when(mine & (f == 0) & (i == n_tiles))
        def _(prev=prev):
            finish(prev)


def _ffn(h2, w1, w2, x1, mod4, g_final, S, tm=512, tf=1024):
    T, D = x1.shape
    F = w1.shape[1]
    per_b = S // tm
    n, nf = T // tm, F // tf
    done = lambda i: jnp.maximum(i - 1, 0)
    chunk = lambda i, f: jnp.where(i == n, nf - 1, f)
    return pl.pallas_call(
        functools.partial(_ffn_kernel, n_tiles=n),
        out_shape=jax.ShapeDtypeStruct((T, D), F32),
        grid=(n + 1, nf),
        in_specs=[pl.BlockSpec((tm, D), lambda i, f: (jnp.minimum(i, n - 1), 0)),
                  pl.BlockSpec((D, tf), lambda i, f: (0, chunk(i, f))),
                  pl.BlockSpec((tf, D), lambda i, f: (chunk(i, f), 0)),
                  pl.BlockSpec((tm, D), lambda i, f: (done(i), 0)),
                  pl.BlockSpec((None, None, 1, D), lambda i, f: (done(i) // per_b, 5, 0, 0)),
                  pl.BlockSpec((1, D), lambda i, f: (0, 0))],
        out_specs=pl.BlockSpec((tm, D), lambda i, f: (done(i), 0)),
        scratch_shapes=[pltpu.VMEM((tm, D), F32), pltpu.VMEM((tm, D), F32)],
        compiler_params=_params(("arbitrary", "arbitrary")),
        name="ffn",
    )(h2, w1, w2, x1, mod4, g_final)


def _pad_to(a, rows, cols):
    return jnp.pad(a, ((0, rows - a.shape[0]), (0, cols - a.shape[1])))


def kernel(x, c, w_ada, b_ada, g_norm1, w_in, w_gla_a_up, b_gla_a, g_gla_out, b_fox_f, w_proj_gla,
           w_proj_fox, w_out, g_norm2, w_ff1, w_ff2, g_final):
    B, S, D = x.shape
    assert w_ada.shape[0] == 1, "the final norm is fused into the last layer's MLP kernel"
    l = 0
    x2d = x.reshape(B * S, D)
    mod4 = _ada(c, w_ada[l], b_ada[l][None, :]).reshape(B, N_MOD, 1, D)

    proj, low = _in_proj(x2d, g_norm1[l][None, :], mod4, jnp.swapaxes(w_in[l], 0, 1), S)

    wup_pad = _pad_to(w_gla_a_up[l], LANES, D // 2).astype(BF16)
    o_a = _gla(proj, low, wup_pad, b_gla_a[l][None, :], g_gla_out[l][None, :], B, S, D)

    bf_pad = _pad_to(b_fox_f[l][None, :], 1, LANES)
    cum_col, cum_row = _fox_cum(low, bf_pad, B, S)
    o_b = _fox(proj, cum_col, cum_row, B, S, D)

    merged = _merge(o_a, o_b, w_proj_gla[l].astype(BF16), w_proj_fox[l].astype(BF16), proj, D)
    x1, h2 = _out_proj(merged, w_out[l].astype(BF16), x2d, mod4, g_norm2[l][None, :], S)
    out = _ffn(h2, w_ff1[l].astype(BF16), w_ff2[l].astype(BF16), x1, mod4, g_final[None, :], S)
    return out.reshape(B, S, D)
```
